```python
import math
import jax, jax.numpy as jnp
from jax import lax
import numpy as np

D_MODEL = 1024
BATCH = 16
SEQ = 4096
DEPTH = 2

D_PLE = 256
D_FF = 2816
D_CONV = 512
CONV_W = 3
H_M = 4
DQK_M = 128
DV_M = 256
D_MV = H_M * DV_M
CHUNK = 64
H_A = 8
DH_A = 64
H_IDX = 4
D_IDX = 64
TOPK_MAX = 256
Q_BLOCK = 128
EPS = 1e-6

IN_SPLITS = (
    D_CONV, D_CONV, D_CONV,
    H_M * DQK_M, H_M * DQK_M, D_MV, D_MV,
    H_M, H_M,
    H_A * DH_A, DH_A, DH_A,
    H_IDX * D_IDX, D_IDX, H_IDX,
    3 * D_MODEL,
)
N_IN = sum(IN_SPLITS)
IN_OFFSETS = tuple(int(o) for o in np.cumsum(IN_SPLITS)[:-1])

kernel_name = "hybrid_conv_mlstm_dsa_macaron"


def rms_norm(x, g):
    xf = x.astype(jnp.float32)
    y = xf * lax.rsqrt(jnp.mean(xf * xf, axis=-1, keepdims=True) + EPS)
    return (y * g.astype(jnp.float32)).astype(x.dtype)


def swiglu(h, w_gu, w_down):
    g, u = jnp.split(h @ w_gu, 2, axis=-1)
    return (jax.nn.silu(g) * u) @ w_down


def short_conv_mixer(b_gate, c_gate, xc, conv_w, w_out):
    u = c_gate * xc
    S = u.shape[1]
    up = jnp.pad(u, ((0, 0), (CONV_W - 1, 0), (0, 0)))
    y = up[:, 0:S] * conv_w[0]
    for j in range(1, CONV_W):
        y = y + up[:, j:j + S] * conv_w[j]
    return (b_gate * y) @ w_out


def mlstm_chunkwise(q, k, v, i_pre, f_pre):
    B, S, H, dqk = q.shape
    dv = v.shape[-1]
    nc = S // CHUNK
    f32 = jnp.float32
    qf = q.astype(f32)
    kf = k.astype(f32) * (dqk ** -0.5)
    vf = v.astype(f32)
    ig = i_pre.astype(f32)
    lf = jax.nn.log_sigmoid(f_pre.astype(f32))

    def to_chunks(a):
        a = a.reshape((B, nc, CHUNK, H) + a.shape[3:])
        return jnp.moveaxis(a, (1, 3), (0, 2))

    causal = jnp.tril(jnp.ones((CHUNK, CHUNK), dtype=bool))

    def step(carry, inp):
        C, n, m = carry
        qc, kc, vc, ic, lfc = inp
        b = jnp.cumsum(lfc, axis=-1)
        Dlog = jnp.where(causal, b[..., :, None] - b[..., None, :] + ic[..., None, :], -jnp.inf)
        inter = b + m[..., None]
        m_t = jnp.maximum(inter, jnp.max(Dlog, axis=-1))
        sc = jnp.einsum('bhtd,bhsd->bhts', qc, kc) * jnp.exp(Dlog - m_t[..., None])
        w_inter = jnp.exp(inter - m_t)
        num = w_inter[..., None] * jnp.einsum('bhvd,bhtd->bhtv', C, qc) + jnp.einsum('bhts,bhsv->bhtv', sc, vc)
        den = w_inter * jnp.einsum('bhd,bhtd->bht', n, qc) + jnp.sum(sc, axis=-1)
        h = num / jnp.maximum(jnp.abs(den), jnp.exp(-m_t))[..., None]
        bL = b[..., -1]
        g = bL[..., None] - b + ic
        m_new = jnp.maximum(bL + m, jnp.max(g, axis=-1))
        wk = jnp.exp(g - m_new[..., None])
        decay = jnp.exp(bL + m - m_new)
        C_new = decay[..., None, None] * C + jnp.einsum('bhs,bhsv,bhsd->bhvd', wk, vc, kc)
        n_new = decay[..., None] * n + jnp.einsum('bhs,bhsd->bhd', wk, kc)
        return (C_new, n_new, m_new), h

    init = (jnp.zeros((B, H, dv, dqk), f32), jnp.zeros((B, H, dqk), f32), jnp.zeros((B, H), f32))
    _, h = lax.scan(step, init, (to_chunks(qf), to_chunks(kf), to_chunks(vf), to_chunks(ig), to_chunks(lf)))
    h = jnp.moveaxis(h, (0, 2), (1, 3))
    return h.reshape(B, S, H, dv)


def dsa_sparse_attention(q, k, v, q_idx, k_idx, w_idx):
    B, S = q.shape[:2]
    n_sel = min(TOPK_MAX, S // 4)
    nb = S // Q_BLOCK
    f32 = jnp.float32
    kf, vf, kif = k.astype(f32), v.astype(f32), k_idx.astype(f32)
    wf = w_idx.astype(f32) * (H_IDX ** -0.5) * (D_IDX ** -0.5)
    key_pos = jnp.arange(S)
    gather = jax.vmap(lambda a, i: a[i])

    def blocks(a):
        return jnp.moveaxis(a.reshape((B, nb, Q_BLOCK) + a.shape[2:]), 1, 0)

    def attend_block(inp):
        qb, qib, wb, t0 = inp
        qpos = t0 + jnp.arange(Q_BLOCK)
        logits = jnp.einsum('bqhd,bsd->bqhs', qib.astype(f32), kif)
        score = jnp.einsum('bqh,bqhs->bqs', wb, jax.nn.relu(logits))
        visible = key_pos[None, :] <= qpos[:, None]
        score = jnp.where(visible[None], score, -jnp.inf)
        _, idx = lax.top_k(score, n_sel)
        k_sel = gather(kf, idx)
        v_sel = gather(vf, idx)
        s = jnp.einsum('bqhd,bqkd->bqhk', qb.astype(f32), k_sel) * (DH_A ** -0.5)
        valid = idx <= qpos[None, :, None]
        s = jnp.where(valid[:, :, None, :], s, -jnp.inf)
        pr = jax.nn.softmax(s, axis=-1)
        return jnp.einsum('bqhk,bqkd->bqhd', pr, v_sel)

    out = lax.map(attend_block, (blocks(q), blocks(q_idx), blocks(wf), jnp.arange(nb) * Q_BLOCK))
    return jnp.moveaxis(out, 0, 1).reshape(B, S, H_A * DH_A)


def hybrid_mixer(h, w_in, conv_w, conv_w_out, mlstm_b_i, mlstm_b_f, mlstm_norm, mlstm_w_out, attn_w_out, w_o):
    B, S, _ = h.shape
    proj = h @ w_in
    (cb, cc, cx, mq, mk, mv, mo, mi, mf, aq, ak, av, iq, ik, iw, gates) = jnp.split(proj, IN_OFFSETS, axis=-1)
    y_a = short_conv_mixer(cb, cc, cx, conv_w, conv_w_out)
    hm = mlstm_chunkwise(mq.reshape(B, S, H_M, DQK_M), mk.reshape(B, S, H_M, DQK_M),
                         mv.reshape(B, S, H_M, DV_M), mi + mlstm_b_i, mf + mlstm_b_f)
    hm = hm * lax.rsqrt(jnp.mean(hm * hm, axis=-1, keepdims=True) + EPS)
    hm = hm.reshape(B, S, D_MV) * mlstm_norm.astype(jnp.float32)
    y_m = (jax.nn.sigmoid(mo) * hm.astype(h.dtype)) @ mlstm_w_out
    ha = dsa_sparse_attention(aq.reshape(B, S, H_A, DH_A), ak, av,
                              iq.reshape(B, S, H_IDX, D_IDX), ik, iw)
    y_c = ha.astype(h.dtype) @ attn_w_out
    g = jax.nn.sigmoid(gates).reshape(B, S, 3, D_MODEL)
    merged = g[:, :, 0] * y_a + g[:, :, 1] * y_m + g[:, :, 2] * y_c
    return merged @ w_o


def setup_inputs(seed: int = 0) -> dict:
    key = jax.random.key(seed)
    ks = iter(jax.random.split(key, 32))
    f32 = jnp.float32

    def nrm(shape, fan_in):
        return jax.random.normal(next(ks), shape, f32) * (fan_in ** -0.5)

    def gain(shape):
        return 1.0 + 0.02 * jax.random.normal(next(ks), shape, f32)

    return {
        "x": jax.random.normal(next(ks), (BATCH, SEQ, D_MODEL), f32),
        "p": jax.random.normal(next(ks), (DEPTH, BATCH, SEQ, D_PLE), f32),
        "norm_ffn1": gain((DEPTH, D_MODEL)),
        "ffn1_w_gu": nrm((DEPTH, D_MODEL, 2 * D_FF), D_MODEL),
        "ffn1_w_down": nrm((DEPTH, D_FF, D_MODEL), D_FF),
        "norm_mix": gain((DEPTH, D_MODEL)),
        "w_in": nrm((DEPTH, D_MODEL, N_IN), D_MODEL),
        "conv_w": nrm((DEPTH, CONV_W, D_CONV), CONV_W),
        "conv_w_out": nrm((DEPTH, D_CONV, D_MODEL), D_CONV),
        "mlstm_b_i": 0.1 * jax.random.normal(next(ks), (DEPTH, H_M), f32),
        "mlstm_b_f": jnp.linspace(3.0, 6.0, H_M, dtype=f32)[None, :] + 0.1 * jax.random.normal(next(ks), (DEPTH, H_M), f32),
        "mlstm_norm": gain((DEPTH, D_MV)),
        "mlstm_w_out": nrm((DEPTH, D_MV, D_MODEL), D_MV),
        "attn_w_out": nrm((DEPTH, H_A * DH_A, D_MODEL), H_A * DH_A),
        "w_o": nrm((DEPTH, D_MODEL, D_MODEL), D_MODEL),
        "norm_ffn2": gain((DEPTH, D_MODEL)),
        "ffn2_w_gu": nrm((DEPTH, D_MODEL, 2 * D_FF), D_MODEL),
        "ffn2_w_down": nrm((DEPTH, D_FF, D_MODEL), D_FF),
        "norm_ple": gain((DEPTH, D_MODEL)),
        "ple_w_gate": nrm((DEPTH, D_MODEL, D_MODEL), D_MODEL),
        "ple_w_proj": nrm((DEPTH, D_PLE, D_MODEL), D_PLE),
        "final_norm": gain((D_MODEL,)),
    }


def reference(x, p, norm_ffn1, ffn1_w_gu, ffn1_w_down, norm_mix, w_in, conv_w, conv_w_out,
              mlstm_b_i, mlstm_b_f, mlstm_norm, mlstm_w_out, attn_w_out, w_o,
              norm_ffn2, ffn2_w_gu, ffn2_w_down, norm_ple, ple_w_gate, ple_w_proj, final_norm):
    for l in range(DEPTH):
        x = x + 0.5 * swiglu(rms_norm(x, norm_ffn1[l]), ffn1_w_gu[l], ffn1_w_down[l])
        x = x + hybrid_mixer(rms_norm(x, norm_mix[l]), w_in[l], conv_w[l], conv_w_out[l],
                             mlstm_b_i[l], mlstm_b_f[l], mlstm_norm[l], mlstm_w_out[l],
                             attn_w_out[l], w_o[l])
        x = x + 0.5 * swiglu(rms_norm(x, norm_ffn2[l]), ffn2_w_gu[l], ffn2_w_down[l])
        gate = jax.nn.sigmoid(rms_norm(x, norm_ple[l]) @ ple_w_gate[l])
        x = x + gate * (p[l] @ ple_w_proj[l])
    return rms_norm(x, final_norm)
```

```python
import functools

import jax
import jax.numpy as jnp
from jax import lax
from jax.experimental import pallas as pl
from jax.experimental.pallas import tpu as pltpu

BF16 = jnp.bfloat16
F32 = jnp.float32
I32 = jnp.int32

D_PLE = 256
D_FF = 2816
D_CONV = 512
CONV_W = 3
H_M = 4
DQK_M = 128
DV_M = 256
D_MV = H_M * DV_M
H_A = 8
DH_A = 64
H_IDX = 4
D_IDX = 64
TOPK_MAX = 256
Q_BLOCK = 128
EPS = 1e-6

_SPLITS = (D_CONV, D_CONV, D_CONV, H_M * DQK_M, H_M * DQK_M, D_MV, D_MV, H_M, H_M,
           H_A * DH_A, DH_A, DH_A, H_IDX * D_IDX, D_IDX, H_IDX)
_OFF = [0]
for _s in _SPLITS:
    _OFF.append(_OFF[-1] + _s)
(O_CB, O_CC, O_CX, O_MQ, O_MK, O_MV, O_MO, O_MI, O_MF, O_AQ, O_AK, O_AV, O_IQ, O_IK, O_IW, O_GATES) = _OFF

LANES = 128
VMEM_LIMIT_BYTES = 56 * 1024 * 1024

TM_FFN = 512
TM_MIX = 512
FF_CHUNK = 512
MLSTM_L = 256
DSA_KC = 512
INT_MIN = -2 ** 31
NEG_BIG = -1e30


def _const_spec(shape):
    nd = len(shape)
    return pl.BlockSpec(shape, lambda *_: (0,) * nd, pipeline_mode=pl.Buffered(1))


def _rms(x, g):
    return x * lax.rsqrt(jnp.mean(x * x, axis=-1, keepdims=True) + EPS) * g


def _dot(a, b):
    return jnp.dot(a, b, preferred_element_type=F32)


def _dot_nt(a, b):
    return lax.dot_general(a, b, (((1,), (1,)), ((), ())), preferred_element_type=F32)


def _ffn_kernel(*refs, chunks, d_ff, with_ple, with_final):
    if with_ple:
        x_ref, g_ref, wgu_ref, wd_ref, p_ref, gp_ref, wpg_ref, wpp_ref = refs[:8]
        rest = refs[8:]
    else:
        x_ref, g_ref, wgu_ref, wd_ref = refs[:4]
        rest = refs[4:]
    if with_final:
        gf_ref, o_ref = rest
    else:
        (o_ref,) = rest
    x = x_ref[...]
    h = _rms(x, g_ref[...]).astype(BF16)
    acc = jnp.zeros_like(x)
    for off, w in chunks:
        gg = _dot(h, wgu_ref[:, off:off + w])
        uu = _dot(h, wgu_ref[:, d_ff + off:d_ff + off + w])
        a = (gg * jax.nn.sigmoid(gg) * uu).astype(BF16)
        acc = acc + _dot(a, wd_ref[off:off + w, :])
    x = x + 0.5 * acc
    if with_ple:
        gate = jax.nn.sigmoid(_dot(_rms(x, gp_ref[...]).astype(BF16), wpg_ref[...]))
        x = x + gate * _dot(p_ref[...].astype(BF16), wpp_ref[...])
    if with_final:
        x = _rms(x, gf_ref[...])
    o_ref[...] = x


def _ffn(x, g, wgu, wd, ple=None, final_g=None):
    t, d = x.shape
    d_ff = wd.shape[0]
    tm = min(TM_FFN, t)
    chunks = []
    off = 0
    while off < d_ff:
        w = min(FF_CHUNK, d_ff - off)
        chunks.append((off, w))
        off += w
    tok = lambda i: (i, 0)
    in_specs = [pl.BlockSpec((tm, d), tok), _const_spec((1, d)), _const_spec(wgu.shape), _const_spec(wd.shape)]
    args = [x, g.reshape(1, d), wgu, wd]
    if ple is not None:
        p, gp, wpg, wpp = ple
        in_specs += [pl.BlockSpec((tm, p.shape[1]), tok), _const_spec((1, d)), _const_spec(wpg.shape),
                     _const_spec(wpp.shape)]
        args += [p, gp.reshape(1, d), wpg, wpp]
    if final_g is not None:
        in_specs.append(_const_spec((1, d)))
        args.append(final_g.reshape(1, d))
    kern = functools.partial(_ffn_kernel, chunks=tuple(chunks), d_ff=d_ff, with_ple=ple is not None,
                             with_final=final_g is not None)
    return pl.pallas_call(
        kern,
        grid=(t // tm,),
        in_specs=in_specs,
        out_specs=pl.BlockSpec((tm, d), tok),
        out_shape=jax.ShapeDtypeStruct((t, d), F32),
        compiler_params=pltpu.CompilerParams(dimension_semantics=("arbitrary",),
                                             vmem_limit_bytes=VMEM_LIMIT_BYTES),
        name="ffn",
    )(*args)


def _mix_in_kernel(x_ref, g_ref, wconv_ref, wmqk_ref, wmv_ref, wmo_ref, wsmall_ref, waq_ref, wkvi_ref, wiq_ref,
                   wgate_ref, cw_ref, cwo_ref,
                   ya_ref, mqk_ref, mv_ref, smo_ref, small_ref, aq_ref, kvi_ref, iq_ref, g12_ref,
                   ubuf_ref, *, tiles_per_seq):
    tm, d = x_ref.shape
    h = _rms(x_ref[...], g_ref[...]).astype(BF16)

    mqk_ref[...] = _dot(h, wmqk_ref[...]).astype(BF16)
    mv_ref[...] = _dot(h, wmv_ref[...]).astype(BF16)
    smo_ref[...] = jax.nn.sigmoid(_dot(h, wmo_ref[...])).astype(BF16)
    small_ref[...] = _dot(h, wsmall_ref[...])
    aq_ref[...] = _dot(h, waq_ref[...]).astype(BF16)
    kvi_ref[...] = _dot(h, wkvi_ref[...]).astype(BF16)
    iq_ref[...] = _dot(h, wiq_ref[...]).astype(BF16)
    g12_ref[...] = jax.nn.sigmoid(_dot(h, wgate_ref[:, d:3 * d])).astype(BF16)

    @pl.when(pl.program_id(0) % tiles_per_seq == 0)
    def _():
        ubuf_ref[0:8, :] = jnp.zeros((8, D_CONV), F32)

    cb = _dot(h, wconv_ref[:, 0:D_CONV])
    u = _dot(h, wconv_ref[:, D_CONV:2 * D_CONV]) * _dot(h, wconv_ref[:, 2 * D_CONV:3 * D_CONV])
    ubuf_ref[8:tm + 8, :] = u
    y = (ubuf_ref[6:tm + 6, :] * cw_ref[0:1, :] + ubuf_ref[7:tm + 7, :] * cw_ref[1:2, :] + u * cw_ref[2:3, :])
    ubuf_ref[0:8, :] = ubuf_ref[tm:tm + 8, :]
    ya = _dot((cb * y).astype(BF16), cwo_ref[...])
    ya_ref[...] = (jax.nn.sigmoid(_dot(h, wgate_ref[:, 0:d])) * ya).astype(BF16)


def _mix_in(x, g, w, conv_w, conv_w_out, seq):
    t, d = x.shape
    tm = min(TM_MIX, seq)
    tok = lambda i: (i, 0)
    weights = [w["conv"], w["mqk"], w["mv"], w["mo"], w["small"], w["aq"], w["kvi"], w["iq"], w["gates"],
               conv_w, conv_w_out]
    outs = [("ya", d, BF16), ("mqk", 2 * H_M * DQK_M, BF16), ("mv", D_MV, BF16), ("smo", D_MV, BF16),
            ("small", LANES, F32), ("aq", H_A * LANES, BF16), ("kvi", 3 * LANES, BF16), ("iq", H_IDX * LANES, BF16),
            ("g12", 2 * d, BF16)]
    res = pl.pallas_call(
        functools.partial(_mix_in_kernel, tiles_per_seq=seq // tm),
        grid=(t // tm,),
        in_specs=[pl.BlockSpec((tm, d), tok), _const_spec((1, d))] + [_const_spec(a.shape) for a in weights],
        out_specs=[pl.BlockSpec((tm, n), tok) for _, n, _ in outs],
        out_shape=[jax.ShapeDtypeStruct((t, n), dt) for _, n, dt in outs],
        scratch_shapes=[pltpu.VMEM((tm + 8, D_CONV), F32)],
        compiler_params=pltpu.CompilerParams(dimension_semantics=("arbitrary",),
                                             vmem_limit_bytes=VMEM_LIMIT_BYTES),
        name="mix_in",
    )(x, g.reshape(1, d), *weights)
    return {name: r for (name, _, _), r in zip(outs, res)}


def _log_sigmoid(x):
    return jnp.minimum(x, 0.0) - jnp.log(1.0 + jnp.exp(-jnp.abs(x)))


def _mlstm_kernel(qk_ref, v_ref, smo_ref, small_ref, bias_ref, nw_ref, tri_ref, o_ref, ct_ref, n_ref, m_ref):
    L = qk_ref.shape[0]
    scale = DQK_M ** -0.5

    @pl.when(pl.program_id(1) == 0)
    def _():
        ct_ref[...] = jnp.zeros(ct_ref.shape, F32)
        n_ref[...] = jnp.zeros(n_ref.shape, F32)
        m_ref[...] = jnp.zeros(m_ref.shape, F32)

    sm = small_ref[...] + bias_ref[...]
    lane = lax.broadcasted_iota(I32, sm.shape, 1)
    g = jnp.where((lane >= H_M) & (lane < 2 * H_M), _log_sigmoid(sm), sm)
    tri = tri_ref[...]
    g1 = g.astype(BF16)
    r1 = g - g1.astype(F32)
    g2 = r1.astype(BF16)
    g3 = (r1 - g2.astype(F32)).astype(BF16)
    cs = _dot(tri, g1) + _dot(tri, g2) + _dot(tri, g3)
    g_t = g.T
    cs_t = cs.T
    row = lax.broadcasted_iota(I32, (L, L), 0)
    col = lax.broadcasted_iota(I32, (L, L), 1)
    causal = col <= row

    for h in range(H_M):
        i_c = g[:, h:h + 1]
        b_c = cs[:, H_M + h:H_M + h + 1]
        a_r = g_t[h:h + 1, :] - cs_t[H_M + h:H_M + h + 1, :]
        m_prev = m_ref[h:h + 1, 0:1]
        dlog = jnp.where(causal, b_c + a_r, -jnp.inf)
        inter = b_c + m_prev
        m_t = jnp.maximum(inter, jnp.max(dlog, axis=1, keepdims=True))
        dm = jnp.exp(dlog - m_t)
        q = qk_ref[:, h * DQK_M:(h + 1) * DQK_M]
        k = qk_ref[:, (H_M + h) * DQK_M:(H_M + h + 1) * DQK_M]
        v = v_ref[:, h * DV_M:(h + 1) * DV_M]
        sc = _dot_nt(q, k) * scale * dm
        w_int = jnp.exp(inter - m_t)
        ct = ct_ref[h]
        num = w_int * _dot(q, ct.astype(BF16)) + _dot(sc.astype(BF16), v)
        n_row = n_ref[h:h + 1, :]
        qn = jnp.sum(q.astype(F32) * n_row, axis=1, keepdims=True)
        den = w_int * qn + jnp.sum(sc, axis=1, keepdims=True)
        hv = num / jnp.maximum(jnp.abs(den), jnp.exp(-m_t))
        hn = hv * lax.rsqrt(jnp.mean(hv * hv, axis=-1, keepdims=True) + EPS) * nw_ref[:, h * DV_M:(h + 1) * DV_M]
        o_ref[:, h * DV_M:(h + 1) * DV_M] = (smo_ref[:, h * DV_M:(h + 1) * DV_M].astype(F32) * hn).astype(BF16)
        b_last = cs[L - 1:L, H_M + h:H_M + h + 1]
        g_c = b_last - b_c + i_c
        m_new = jnp.maximum(b_last + m_prev, jnp.max(g_c, axis=0, keepdims=True))
        wk = jnp.exp(g_c - m_new) * scale
        decay = jnp.exp(b_last + m_prev - m_new)
        kw = k.astype(F32) * wk
        ct_ref[h] = decay * ct + _dot(kw.T.astype(BF16), v)
        n_ref[h:h + 1, :] = decay * n_row + jnp.sum(kw, axis=0, keepdims=True)
        m_ref[h:h + 1, :] = jnp.broadcast_to(m_new, (1, LANES))


def _mlstm(mqk, mv, smo, small, bias_row, norm_w, batch, seq):
    L = min(MLSTM_L, seq)
    r3 = lambda a: a.reshape(batch, seq, a.shape[-1])
    blk = lambda n: pl.BlockSpec((None, L, n), lambda b, c: (b, c, 0))
    tri = jnp.tril(jnp.ones((L, L), BF16))
    out = pl.pallas_call(
        _mlstm_kernel,
        grid=(batch, seq // L),
        in_specs=[blk(2 * H_M * DQK_M), blk(D_MV), blk(D_MV), blk(LANES), _const_spec((1, LANES)),
                  _const_spec((1, D_MV)), _const_spec((L, L))],
        out_specs=blk(D_MV),
        out_shape=jax.ShapeDtypeStruct((batch, seq, D_MV), BF16),
        scratch_shapes=[pltpu.VMEM((H_M, DQK_M, DV_M), F32), pltpu.VMEM((8, LANES), F32),
                        pltpu.VMEM((8, LANES), F32)],
        compiler_params=pltpu.CompilerParams(dimension_semantics=("arbitrary", "arbitrary"),
                                             vmem_limit_bytes=VMEM_LIMIT_BYTES),
        name="mlstm",
    )(r3(mqk), r3(mv), r3(smo), r3(small), bias_row, norm_w.reshape(1, D_MV), tri)
    return out.reshape(batch * seq, D_MV)


def _dsa_kernel(aq_ref, iq_ref, small_ref, kvi_ref, tri_ref, wout_ref, o_ref,
                keys_ref, p_ref, acc_ref, m_ref, l_ref, *, n_sel):
    kc = keys_ref.shape[2]
    nlg = kc // LANES
    qb = pl.program_id(1)
    nkc = (qb * Q_BLOCK) // kc + 1
    t_idx = qb * Q_BLOCK + lax.broadcasted_iota(I32, (Q_BLOCK, kc), 0)
    s_iota = lax.broadcasted_iota(I32, (Q_BLOCK, kc), 1)

    iq_all = jnp.concatenate([iq_ref[:, h * LANES:(h + 1) * LANES] for h in range(H_IDX)], axis=0)
    w_scale = (H_IDX ** -0.5) * (D_IDX ** -0.5)
    wb = [jnp.broadcast_to(small_ref[:, 2 * H_M + h:2 * H_M + h + 1] * w_scale, (Q_BLOCK, kc))
          for h in range(H_IDX)]

    def score_body(c, carry):
        start = pl.multiple_of(c * kc, kc)
        ik_c = kvi_ref[pl.ds(start, kc), 2 * LANES:3 * LANES]
        lg = _dot_nt(iq_all, ik_c)
        sc = wb[0] * jnp.maximum(lg[0:Q_BLOCK], 0.0)
        for h in range(1, H_IDX):
            sc = sc + wb[h] * jnp.maximum(lg[h * Q_BLOCK:(h + 1) * Q_BLOCK], 0.0)
        bits = lax.bitcast_convert_type(sc, I32)
        bits = jnp.where(bits == INT_MIN, 0, bits)
        key = bits ^ ((bits >> 31) & 0x7FFFFFFF)
        keys_ref[c] = jnp.where(c * kc + s_iota <= t_idx, key, INT_MIN)
        return carry

    lax.fori_loop(0, nkc, score_body, 0)

    def count_rows(pred):
        def body(c, acc):
            k = keys_ref[c]
            for j in range(nlg):
                acc = acc + jnp.where(pred(k[:, j * LANES:(j + 1) * LANES]), 1.0, 0.0)
            return acc
        acc = lax.fori_loop(0, nkc, body, jnp.zeros((Q_BLOCK, LANES), F32))
        return jnp.sum(acc, axis=1, keepdims=True)

    def bit_body(i, t_acc):
        cand = t_acc | jnp.left_shift(jnp.int32(1), 31 - i)
        cmp_w = jnp.broadcast_to(cand ^ INT_MIN, (Q_BLOCK, LANES))
        cnt = count_rows(lambda k: k >= cmp_w)
        return jnp.where(cnt >= n_sel, cand, t_acc)

    t_bits = lax.fori_loop(0, 32, bit_body, jnp.zeros((Q_BLOCK, 1), I32))
    thr = t_bits ^ INT_MIN
    thr_w = jnp.broadcast_to(thr, (Q_BLOCK, LANES))
    need = n_sel - count_rows(lambda k: k > thr_w)
    thr_k = jnp.broadcast_to(thr, (Q_BLOCK, kc))
    need_k = jnp.broadcast_to(need, (Q_BLOCK, kc))

    q_all = jnp.concatenate([aq_ref[:, h * LANES:(h + 1) * LANES] for h in range(H_A)], axis=0)
    m_ref[...] = jnp.full(m_ref.shape, NEG_BIG, F32)
    l_ref[...] = jnp.zeros(l_ref.shape, F32)
    acc_ref[...] = jnp.zeros(acc_ref.shape, F32)
    tri = tri_ref[...]

    def att_body(c, tie_carry):
        start = pl.multiple_of(c * kc, kc)
        k = keys_ref[c]
        eq = k == thr_k
        eq_f = jnp.where(eq, 1.0, 0.0)
        rank = _dot(eq_f.astype(BF16), tri) + tie_carry
        sel = ((k > thr_k) | (eq & (rank < need_k))) & (k > INT_MIN)
        tie_carry = tie_carry + jnp.sum(eq_f, axis=1, keepdims=True)
        k_c = kvi_ref[pl.ds(start, kc), 0:LANES]
        v_c = kvi_ref[pl.ds(start, kc), LANES:2 * LANES]
        s_all = _dot_nt(q_all, k_c)
        for h in range(H_A):
            s = jnp.where(sel, s_all[h * Q_BLOCK:(h + 1) * Q_BLOCK], -jnp.inf)
            m_old = m_ref[h]
            m_new = jnp.maximum(m_old, jnp.max(s, axis=1, keepdims=True))
            p = jnp.exp(s - m_new)
            alpha = jnp.exp(m_old - m_new)
            l_ref[h] = alpha * l_ref[h] + jnp.sum(p, axis=1, keepdims=True)
            m_ref[h] = m_new
            p_ref[h * Q_BLOCK:(h + 1) * Q_BLOCK, :] = p.astype(BF16)
            acc_ref[h] = alpha * acc_ref[h]
        pv = _dot(p_ref[...], v_c)
        for h in range(H_A):
            acc_ref[h] = acc_ref[h] + pv[h * Q_BLOCK:(h + 1) * Q_BLOCK]
        return tie_carry

    lax.fori_loop(0, nkc, att_body, jnp.zeros((Q_BLOCK, 1), F32))

    yc = jnp.zeros(o_ref.shape, F32)
    for h in range(H_A):
        yc = yc + _dot((acc_ref[h] / l_ref[h]).astype(BF16), wout_ref[h])
    o_ref[...] = yc.astype(BF16)


def _dsa(aq, iq, small, kvi, wout_p, batch, seq):
    d = wout_p.shape[-1]
    kc = min(DSA_KC, seq)
    n_sel = min(TOPK_MAX, seq // 4)
    r3 = lambda a: a.reshape(batch, seq, a.shape[-1])
    qblk = lambda n: pl.BlockSpec((None, Q_BLOCK, n), lambda b, q: (b, q, 0))
    tri = jnp.triu(jnp.ones((kc, kc), BF16), k=1)
    out = pl.pallas_call(
        functools.partial(_dsa_kernel, n_sel=n_sel),
        grid=(batch, seq // Q_BLOCK),
        in_specs=[qblk(H_A * LANES), qblk(H_IDX * LANES), qblk(LANES),
                  pl.BlockSpec((None, seq, 3 * LANES), lambda b, q: (b, 0, 0)),
                  _const_spec((kc, kc)), _const_spec(wout_p.shape)],
        out_specs=qblk(d),
        out_shape=jax.ShapeDtypeStruct((batch, seq, d), BF16),
        scratch_shapes=[pltpu.VMEM((seq // kc, Q_BLOCK, kc), I32),
                        pltpu.VMEM((H_A * Q_BLOCK, kc), BF16),
                        pltpu.VMEM((H_A, Q_BLOCK, LANES), F32),
                        pltpu.VMEM((H_A, Q_BLOCK, 1), F32),
                        pltpu.VMEM((H_A, Q_BLOCK, 1), F32)],
        compiler_params=pltpu.CompilerParams(dimension_semantics=("arbitrary", "arbitrary"),
                                             vmem_limit_bytes=VMEM_LIMIT_BYTES),
        name="dsa",
    )(r3(aq), r3(iq), r3(small), r3(kvi), tri, wout_p)
    return out.reshape(batch * seq, d)


def _mix_out_kernel(x_ref, ya_ref, hm_ref, yc_ref, g12_ref, wm_ref, wo_ref, o_ref):
    d = x_ref.shape[1]
    ym = _dot(hm_ref[...], wm_ref[...])
    merged = (ya_ref[...].astype(F32) + g12_ref[:, 0:d].astype(F32) * ym
              + g12_ref[:, d:2 * d].astype(F32) * yc_ref[...].astype(F32))
    o_ref[...] = x_ref[...] + _dot(merged.astype(BF16), wo_ref[...])


def _mix_out(x, ya, hm, yc, g12, wm, wo):
    t, d = x.shape
    tm = min(TM_FFN, t)
    tok = lambda i: (i, 0)
    return pl.pallas_call(
        _mix_out_kernel,
        grid=(t // tm,),
        in_specs=[pl.BlockSpec((tm, d), tok), pl.BlockSpec((tm, d), tok), pl.BlockSpec((tm, D_MV), tok),
                  pl.BlockSpec((tm, d), tok), pl.BlockSpec((tm, 2 * d), tok), _const_spec(wm.shape),
                  _const_spec(wo.shape)],
        out_specs=pl.BlockSpec((tm, d), tok),
        out_shape=jax.ShapeDtypeStruct((t, d), F32),
        compiler_params=pltpu.CompilerParams(dimension_semantics=("arbitrary",),
                                             vmem_limit_bytes=VMEM_LIMIT_BYTES),
        name="mix_out",
    )(x, ya, hm, yc, g12, wm, wo)


def _pad_heads(w, heads, dh):
    d = w.shape[0]
    return jnp.pad(w.reshape(d, heads, dh), ((0, 0), (0, 0), (0, LANES - dh))).reshape(d, heads * LANES)


def _mixer_weights(w_in):
    d = w_in.shape[0]
    wb = w_in.astype(BF16)
    z = lambda n: jnp.zeros((d, n), BF16)
    col = lambda o, n: wb[:, o:o + n]
    return {
        "conv": col(O_CB, 3 * D_CONV),
        "mqk": col(O_MQ, 2 * H_M * DQK_M),
        "mv": col(O_MV, D_MV),
        "mo": col(O_MO, D_MV),
        "small": jnp.concatenate([col(O_MI, 2 * H_M), col(O_IW, H_IDX), z(LANES - 2 * H_M - H_IDX)], axis=1),
        "aq": _pad_heads(col(O_AQ, H_A * DH_A) * jnp.asarray(DH_A ** -0.5, BF16), H_A, DH_A),
        "kvi": jnp.concatenate([col(O_AK, DH_A), z(LANES - DH_A), col(O_AV, DH_A), z(LANES - DH_A),
                                col(O_IK, D_IDX), z(LANES - D_IDX)], axis=1),
        "iq": _pad_heads(col(O_IQ, H_IDX * D_IDX), H_IDX, D_IDX),
        "gates": col(O_GATES, 3 * d),
    }


def kernel(x, p, norm_ffn1, ffn1_w_gu, ffn1_w_down, norm_mix, w_in, conv_w, conv_w_out, mlstm_b_i, mlstm_b_f,
           mlstm_norm, mlstm_w_out, attn_w_out, w_o, norm_ffn2, ffn2_w_gu, ffn2_w_down, norm_ple, ple_w_gate,
           ple_w_proj, final_norm):
    batch, seq, d = x.shape
    depth = w_in.shape[0]
    t = batch * seq
    xs = x.reshape(t, d)
    for l in range(depth):
        xs = _ffn(xs, norm_ffn1[l], ffn1_w_gu[l].astype(BF16), ffn1_w_down[l].astype(BF16))
        pr = _mix_in(xs, norm_mix[l], _mixer_weights(w_in[l]), conv_w[l], conv_w_out[l].astype(BF16), seq)
        bias_row = jnp.concatenate([mlstm_b_i[l], mlstm_b_f[l], jnp.zeros((LANES - 2 * H_M,), F32)]).reshape(1, LANES)
        hm = _mlstm(pr["mqk"], pr["mv"], pr["smo"], pr["small"], bias_row, mlstm_norm[l], batch, seq)
        wout_p = jnp.pad(attn_w_out[l].astype(BF16).reshape(H_A, DH_A, d), ((0, 0), (0, LANES - DH_A), (0, 0)))
        yc = _dsa(pr["aq"], pr["iq"], pr["small"], pr["kvi"], wout_p, batch, seq)
        xs = _mix_out(xs, pr["ya"], hm, yc, pr["g12"], mlstm_w_out[l].astype(BF16), w_o[l].astype(BF16))
        xs = _ffn(xs, norm_ffn2[l], ffn2_w_gu[l].astype(BF16), ffn2_w_down[l].astype(BF16),
                  ple=(p[l].reshape(t, p.shape[-1]), norm_ple[l], ple_w_gate[l].astype(BF16),
                       ple_w_proj[l].astype(BF16)),
                  final_g=final_norm if l == depth - 1 else None)
    return xs.reshape(batch, seq, d)
```

```python
import functools

import jax
import jax.numpy as jnp
from jax import lax
from jax.experimental import pallas as pl
from jax.experimental.pallas import tpu as pltpu

BF16 = jnp.bfloat16
F32 = jnp.float32
I32 = jnp.int32

D_PLE = 256
D_FF = 2816
D_CONV = 512
CONV_W = 3
H_M = 4
DQK_M = 128
DV_M = 256
D_MV = H_M * DV_M
H_A = 8
DH_A = 64
H_IDX = 4
D_IDX = 64
TOPK_MAX = 256
Q_BLOCK = 128
EPS = 1e-6

_SPLITS = (D_CONV, D_CONV, D_CONV, H_M * DQK_M, H_M * DQK_M, D_MV, D_MV, H_M, H_M,
           H_A * DH_A, DH_A, DH_A, H_IDX * D_IDX, D_IDX, H_IDX)
_OFF = [0]
for _s in _SPLITS:
    _OFF.append(_OFF[-1] + _s)
(O_CB, O_CC, O_CX, O_MQ, O_MK, O_MV, O_MO, O_MI, O_MF, O_AQ, O_AK, O_AV, O_IQ, O_IK, O_IW, O_GATES) = _OFF

LANES = 128
VMEM_LIMIT_BYTES = 56 * 1024 * 1024

TM_FFN = 512
TM_MIX = 512
FF_CHUNK = 512
MLSTM_L = 256
DSA_KC = 512
INT_MIN = -2 ** 31
MASK_BIAS = -1e30
M_FLOOR = -1e29
LOG2E = 1.4426950408889634
ACC_ROWS = 64


def _const_spec(shape):
    nd = len(shape)
    return pl.BlockSpec(shape, lambda *_: (0,) * nd, pipeline_mode=pl.Buffered(1))


def _rms(x, g):
    return x * lax.rsqrt(jnp.mean(x * x, axis=-1, keepdims=True) + EPS) * g


def _dot(a, b):
    return jnp.dot(a, b, preferred_element_type=F32)


def _dot_nt(a, b):
    return lax.dot_general(a, b, (((1,), (1,)), ((), ())), preferred_element_type=F32)


def _ffn_kernel(*refs, chunks, d_ff, with_ple, with_final):
    if with_ple:
        x_ref, g_ref, wgu_ref, wd_ref, p_ref, gp_ref, wpg_ref, wpp_ref = refs[:8]
        rest = refs[8:]
    else:
        x_ref, g_ref, wgu_ref, wd_ref = refs[:4]
        rest = refs[4:]
    if with_final:
        gf_ref, o_ref = rest
    else:
        (o_ref,) = rest
    x = x_ref[...]
    h = _rms(x, g_ref[...]).astype(BF16)
    acc = jnp.zeros_like(x)
    for off, w in chunks:
        gg = _dot(h, wgu_ref[:, off:off + w])
        uu = _dot(h, wgu_ref[:, d_ff + off:d_ff + off + w])
        a = (gg * jax.nn.sigmoid(gg) * uu).astype(BF16)
        acc = acc + _dot(a, wd_ref[off:off + w, :])
    x = x + 0.5 * acc
    if with_ple:
        gate = jax.nn.sigmoid(_dot(_rms(x, gp_ref[...]).astype(BF16), wpg_ref[...]))
        x = x + gate * _dot(p_ref[...].astype(BF16), wpp_ref[...])
    if with_final:
        x = _rms(x, gf_ref[...])
    o_ref[...] = x


def _ffn(x, g, wgu, wd, ple=None, final_g=None):
    t, d = x.shape
    d_ff = wd.shape[0]
    tm = min(TM_FFN, t)
    chunks = []
    off = 0
    while off < d_ff:
        w = min(FF_CHUNK, d_ff - off)
        chunks.append((off, w))
        off += w
    tok = lambda i: (i, 0)
    in_specs = [pl.BlockSpec((tm, d), tok), _const_spec((1, d)), _const_spec(wgu.shape), _const_spec(wd.shape)]
    args = [x, g.reshape(1, d), wgu, wd]
    if ple is not None:
        p, gp, wpg, wpp = ple
        in_specs += [pl.BlockSpec((tm, p.shape[1]), tok), _const_spec((1, d)), _const_spec(wpg.shape),
                     _const_spec(wpp.shape)]
        args += [p, gp.reshape(1, d), wpg, wpp]
    if final_g is not None:
        in_specs.append(_const_spec((1, d)))
        args.append(final_g.reshape(1, d))
    kern = functools.partial(_ffn_kernel, chunks=tuple(chunks), d_ff=d_ff, with_ple=ple is not None,
                             with_final=final_g is not None)
    return pl.pallas_call(
        kern,
        grid=(t // tm,),
        in_specs=in_specs,
        out_specs=pl.BlockSpec((tm, d), tok),
        out_shape=jax.ShapeDtypeStruct((t, d), F32),
        compiler_params=pltpu.CompilerParams(dimension_semantics=("arbitrary",),
                                             vmem_limit_bytes=VMEM_LIMIT_BYTES),
        name="ffn",
    )(*args)


def _mix_in_kernel(x_ref, g_ref, wconv_ref, wmqk_ref, wmv_ref, wmo_ref, wsmall_ref, waq_ref, wkvi_ref, wiq_ref,
                   wgate_ref, cw_ref, cwo_ref,
                   ya_ref, mqk_ref, mv_ref, smo_ref, small_ref, aq_ref, kvi_ref, iq_ref, g12_ref,
                   ubuf_ref, *, tiles_per_seq):
    tm, d = x_ref.shape
    h = _rms(x_ref[...], g_ref[...]).astype(BF16)

    mqk_ref[...] = _dot(h, wmqk_ref[...]).astype(BF16)
    mv_ref[...] = _dot(h, wmv_ref[...]).astype(BF16)
    smo_ref[...] = jax.nn.sigmoid(_dot(h, wmo_ref[...])).astype(BF16)
    small_ref[...] = _dot(h, wsmall_ref[...])
    aq_ref[...] = (_dot(h, waq_ref[...]) * LOG2E).astype(BF16)
    kvi_ref[...] = _dot(h, wkvi_ref[...]).astype(BF16)
    iq_ref[...] = _dot(h, wiq_ref[...]).astype(BF16)
    g12_ref[...] = jax.nn.sigmoid(_dot(h, wgate_ref[:, d:3 * d])).astype(BF16)

    @pl.when(pl.program_id(0) % tiles_per_seq == 0)
    def _():
        ubuf_ref[0:8, :] = jnp.zeros((8, D_CONV), F32)

    cb = _dot(h, wconv_ref[:, 0:D_CONV])
    u = _dot(h, wconv_ref[:, D_CONV:2 * D_CONV]) * _dot(h, wconv_ref[:, 2 * D_CONV:3 * D_CONV])
    ubuf_ref[8:tm + 8, :] = u
    y = (ubuf_ref[6:tm + 6, :] * cw_ref[0:1, :] + ubuf_ref[7:tm + 7, :] * cw_ref[1:2, :] + u * cw_ref[2:3, :])
    ubuf_ref[0:8, :] = ubuf_ref[tm:tm + 8, :]
    ya = _dot((cb * y).astype(BF16), cwo_ref[...])
    ya_ref[...] = (jax.nn.sigmoid(_dot(h, wgate_ref[:, 0:d])) * ya).astype(BF16)


def _mix_in(x, g, w, conv_w, conv_w_out, seq):
    t, d = x.shape
    tm = min(TM_MIX, seq)
    tok = lambda i: (i, 0)
    weights = [w["conv"], w["mqk"], w["mv"], w["mo"], w["small"], w["aq"], w["kvi"], w["iq"], w["gates"],
               conv_w, conv_w_out]
    outs = [("ya", d, BF16), ("mqk", 2 * H_M * DQK_M, BF16), ("mv", D_MV, BF16), ("smo", D_MV, BF16),
            ("small", LANES, F32), ("aq", H_A * LANES, BF16), ("kvi", 3 * LANES, BF16), ("iq", H_IDX * LANES, BF16),
            ("g12", 2 * d, BF16)]
    res = pl.pallas_call(
        functools.partial(_mix_in_kernel, tiles_per_seq=seq // tm),
        grid=(t // tm,),
        in_specs=[pl.BlockSpec((tm, d), tok), _const_spec((1, d))] + [_const_spec(a.shape) for a in weights],
        out_specs=[pl.BlockSpec((tm, n), tok) for _, n, _ in outs],
        out_shape=[jax.ShapeDtypeStruct((t, n), dt) for _, n, dt in outs],
        scratch_shapes=[pltpu.VMEM((tm + 8, D_CONV), F32)],
        compiler_params=pltpu.CompilerParams(dimension_semantics=("arbitrary",),
                                             vmem_limit_bytes=VMEM_LIMIT_BYTES),
        name="mix_in",
    )(x, g.reshape(1, d), *weights)
    return {name: r for (name, _, _), r in zip(outs, res)}


def _log_sigmoid(x):
    return jnp.minimum(x, 0.0) - jnp.log(1.0 + jnp.exp(-jnp.abs(x)))


def _mlstm_kernel(qk_ref, v_ref, smo_ref, small_ref, bias_ref, nw_ref, tri_ref, o_ref, ct_ref, n_ref, m_ref):
    L = qk_ref.shape[0]
    scale = DQK_M ** -0.5

    @pl.when(pl.program_id(1) == 0)
    def _():
        ct_ref[...] = jnp.zeros(ct_ref.shape, F32)
        n_ref[...] = jnp.zeros(n_ref.shape, F32)
        m_ref[...] = jnp.zeros(m_ref.shape, F32)

    sm = small_ref[...] + bias_ref[...]
    lane = lax.broadcasted_iota(I32, sm.shape, 1)
    g = jnp.where((lane >= H_M) & (lane < 2 * H_M), _log_sigmoid(sm), sm)
    tri = tri_ref[...]
    g1 = g.astype(BF16)
    r1 = g - g1.astype(F32)
    g2 = r1.astype(BF16)
    g3 = (r1 - g2.astype(F32)).astype(BF16)
    cs = _dot(tri, g1) + _dot(tri, g2) + _dot(tri, g3)
    g_t = g.T
    cs_t = cs.T
    row = lax.broadcasted_iota(I32, (L, L), 0)
    col = lax.broadcasted_iota(I32, (L, L), 1)
    causal = col <= row

    for h in range(H_M):
        i_c = g[:, h:h + 1]
        b_c = cs[:, H_M + h:H_M + h + 1]
        a_r = g_t[h:h + 1, :] - cs_t[H_M + h:H_M + h + 1, :]
        m_prev = m_ref[h:h + 1, 0:1]
        dlog = jnp.where(causal, b_c + a_r, -jnp.inf)
        inter = b_c + m_prev
        m_t = jnp.maximum(inter, jnp.max(dlog, axis=1, keepdims=True))
        dm = jnp.exp(dlog - m_t)
        q = qk_ref[:, h * DQK_M:(h + 1) * DQK_M]
        k = qk_ref[:, (H_M + h) * DQK_M:(H_M + h + 1) * DQK_M]
        v = v_ref[:, h * DV_M:(h + 1) * DV_M]
        sc = _dot_nt(q, k) * scale * dm
        w_int = jnp.exp(inter - m_t)
        ct = ct_ref[h]
        num = w_int * _dot(q, ct.astype(BF16)) + _dot(sc.astype(BF16), v)
        n_row = n_ref[h:h + 1, :]
        qn = jnp.sum(q.astype(F32) * n_row, axis=1, keepdims=True)
        den = w_int * qn + jnp.sum(sc, axis=1, keepdims=True)
        hv = num / jnp.maximum(jnp.abs(den), jnp.exp(-m_t))
        hn = hv * lax.rsqrt(jnp.mean(hv * hv, axis=-1, keepdims=True) + EPS) * nw_ref[:, h * DV_M:(h + 1) * DV_M]
        o_ref[:, h * DV_M:(h + 1) * DV_M] = (smo_ref[:, h * DV_M:(h + 1) * DV_M].astype(F32) * hn).astype(BF16)
        b_last = cs[L - 1:L, H_M + h:H_M + h + 1]
        g_c = b_last - b_c + i_c
        m_new = jnp.maximum(b_last + m_prev, jnp.max(g_c, axis=0, keepdims=True))
        wk = jnp.exp(g_c - m_new) * scale
        decay = jnp.exp(b_last + m_prev - m_new)
        kw = k.astype(F32) * wk
        ct_ref[h] = decay * ct + _dot(kw.T.astype(BF16), v)
        n_ref[h:h + 1, :] = decay * n_row + jnp.sum(kw, axis=0, keepdims=True)
        m_ref[h:h + 1, :] = jnp.broadcast_to(m_new, (1, LANES))


def _mlstm(mqk, mv, smo, small, bias_row, norm_w, batch, seq):
    L = min(MLSTM_L, seq)
    r3 = lambda a: a.reshape(batch, seq, a.shape[-1])
    blk = lambda n: pl.BlockSpec((None, L, n), lambda b, c: (b, c, 0))
    tri = jnp.tril(jnp.ones((L, L), BF16))
    out = pl.pallas_call(
        _mlstm_kernel,
        grid=(batch, seq // L),
        in_specs=[blk(2 * H_M * DQK_M), blk(D_MV), blk(D_MV), blk(LANES), _const_spec((1, LANES)),
                  _const_spec((1, D_MV)), _const_spec((L, L))],
        out_specs=blk(D_MV),
        out_shape=jax.ShapeDtypeStruct((batch, seq, D_MV), BF16),
        scratch_shapes=[pltpu.VMEM((H_M, DQK_M, DV_M), F32), pltpu.VMEM((8, LANES), F32),
                        pltpu.VMEM((8, LANES), F32)],
        compiler_params=pltpu.CompilerParams(dimension_semantics=("arbitrary", "arbitrary"),
                                             vmem_limit_bytes=VMEM_LIMIT_BYTES),
        name="mlstm",
    )(r3(mqk), r3(mv), r3(smo), r3(small), bias_row, norm_w.reshape(1, D_MV), tri)
    return out.reshape(batch * seq, D_MV)


def _dsa_kernel(aq_ref, iq_ref, small_ref, kvi_ref, tri_ref, eye_ref, wout_t_ref, o_ref,
                keys_ref, p_ref, acc_ref, m_ref, *, n_sel):
    kc = keys_ref.shape[1]
    qb = pl.program_id(1)
    nkc = (qb * Q_BLOCK) // kc + 1
    t_idx = qb * Q_BLOCK + lax.broadcasted_iota(I32, (kc, Q_BLOCK), 1)
    s_iota = lax.broadcasted_iota(I32, (kc, Q_BLOCK), 0)

    def part(x):
        return x.reshape(kc // ACC_ROWS, ACC_ROWS, Q_BLOCK)

    def rows_sum(x):
        return jnp.sum(jnp.sum(part(x), axis=0), axis=0, keepdims=True)

    iq_all = jnp.concatenate([iq_ref[:, h * LANES:(h + 1) * LANES] for h in range(H_IDX)], axis=0)
    w_scale = (H_IDX ** -0.5) * (D_IDX ** -0.5)
    sm_t = small_ref[...].T
    w_rows = [sm_t[2 * H_M + h:2 * H_M + h + 1, :] * w_scale for h in range(H_IDX)]

    def score_body(c, carry):
        start = pl.multiple_of(c * kc, kc)
        ik_c = kvi_ref[pl.ds(start, kc), 2 * LANES:3 * LANES]
        lg = _dot_nt(ik_c, iq_all)
        sc = w_rows[0] * jnp.maximum(lg[:, 0:Q_BLOCK], 0.0)
        for h in range(1, H_IDX):
            sc = sc + w_rows[h] * jnp.maximum(lg[:, h * Q_BLOCK:(h + 1) * Q_BLOCK], 0.0)
        bits = lax.bitcast_convert_type(sc, I32)
        bits = jnp.where(bits == INT_MIN, 0, bits)
        key = bits ^ ((bits >> 31) & 0x7FFFFFFF)
        keys_ref[c] = jnp.where(c * kc + s_iota <= t_idx, key, INT_MIN)
        return carry

    lax.fori_loop(0, nkc, score_body, 0)

    def count_keys(pred):
        def body(c, acc):
            return acc + jnp.sum(part(jnp.where(pred(keys_ref[c]), 1.0, 0.0)), axis=0)
        acc = lax.fori_loop(0, nkc, body, jnp.zeros((ACC_ROWS, Q_BLOCK), F32))
        return jnp.sum(acc, axis=0, keepdims=True)

    def bit_body(i, carry):
        t_acc, n_ge = carry
        cand = t_acc | jnp.left_shift(jnp.int32(1), 31 - i)
        cmp_row = cand ^ INT_MIN
        cnt = count_keys(lambda k: k >= cmp_row)
        ok = cnt >= n_sel
        return jnp.where(ok, cand, t_acc), jnp.where(ok, cnt, n_ge)

    t_bits, n_ge = lax.fori_loop(0, 32, bit_body, (jnp.zeros((1, Q_BLOCK), I32), jnp.zeros((1, Q_BLOCK), F32)))
    thr = t_bits ^ INT_MIN
    need = n_sel - count_keys(lambda k: k > thr)
    excess_ties = jnp.max(n_ge) > n_sel

    q_all = jnp.concatenate([aq_ref[:, h * LANES:(h + 1) * LANES] for h in range(H_A)], axis=0)
    q_aug = jnp.concatenate([q_all, eye_ref[...]], axis=1)
    m_ref[...] = jnp.full(m_ref.shape, M_FLOOR, F32)
    acc_ref[...] = jnp.zeros(acc_ref.shape, F32)
    d_iota = lax.broadcasted_iota(I32, (LANES, kc), 0)

    def att_step(c, tie_carry, with_ties):
        start = pl.multiple_of(c * kc, kc)
        k = keys_ref[c]
        if with_ties:
            eq = k == thr
            eq_f = jnp.where(eq, 1.0, 0.0)
            rank = _dot(tri_ref[...], eq_f.astype(BF16)) + tie_carry
            sel = ((k > thr) | (eq & (rank < need))) & (k > INT_MIN)
            tie_carry = tie_carry + rows_sum(eq_f)
        else:
            sel = (k >= thr) & (k > INT_MIN)
        bias = jnp.where(sel, 0.0, MASK_BIAS).astype(BF16)
        k_aug = jnp.concatenate([kvi_ref[pl.ds(start, kc), 0:LANES], bias], axis=1)
        v_c = kvi_ref[pl.ds(start, kc), LANES:2 * LANES]
        v_t = jnp.where(d_iota == DH_A, 1.0, v_c.astype(F32).T).astype(BF16)
        s_t = _dot_nt(k_aug, q_aug)
        for h in range(H_A):
            cols = slice(h * Q_BLOCK, (h + 1) * Q_BLOCK)
            sh = s_t[:, cols]
            m_old = m_ref[h:h + 1, :]
            m_new = jnp.maximum(m_old, jnp.max(jnp.max(part(sh), axis=0), axis=0, keepdims=True))
            p_ref[:, cols] = jnp.exp2(sh - m_new).astype(BF16)
            m_ref[h:h + 1, :] = m_new
            acc_ref[:, cols] = acc_ref[:, cols] * jnp.exp2(m_old - m_new)
        acc_ref[...] = acc_ref[...] + _dot(v_t, p_ref[...])
        return tie_carry

    @pl.when(excess_ties)
    def _():
        lax.fori_loop(0, nkc, functools.partial(att_step, with_ties=True), jnp.zeros((1, Q_BLOCK), F32))

    @pl.when(jnp.logical_not(excess_ties))
    def _():
        lax.fori_loop(0, nkc, functools.partial(att_step, with_ties=False), jnp.zeros((1, Q_BLOCK), F32))

    yc_t = jnp.zeros((o_ref.shape[1], Q_BLOCK), F32)
    for h in range(H_A):
        cols = slice(h * Q_BLOCK, (h + 1) * Q_BLOCK)
        o_t = acc_ref[:, cols] / acc_ref[DH_A:DH_A + 1, cols]
        yc_t = yc_t + _dot(wout_t_ref[h], o_t.astype(BF16))
    o_ref[...] = yc_t.T.astype(BF16)


def _dsa(aq, iq, small, kvi, wout_t, batch, seq):
    d = wout_t.shape[1]
    kc = min(DSA_KC, seq)
    n_sel = min(TOPK_MAX, seq // 4)
    r3 = lambda a: a.reshape(batch, seq, a.shape[-1])
    qblk = lambda n: pl.BlockSpec((None, Q_BLOCK, n), lambda b, q: (b, q, 0))
    tri = jnp.tril(jnp.ones((kc, kc), BF16), k=-1)
    eye = jnp.tile(jnp.eye(Q_BLOCK, dtype=BF16), (H_A, 1))
    out = pl.pallas_call(
        functools.partial(_dsa_kernel, n_sel=n_sel),
        grid=(batch, seq // Q_BLOCK),
        in_specs=[qblk(H_A * LANES), qblk(H_IDX * LANES), qblk(LANES),
                  pl.BlockSpec((None, seq, 3 * LANES), lambda b, q: (b, 0, 0)),
                  _const_spec((kc, kc)), _const_spec(eye.shape), _const_spec(wout_t.shape)],
        out_specs=qblk(d),
        out_shape=jax.ShapeDtypeStruct((batch, seq, d), BF16),
        scratch_shapes=[pltpu.VMEM((seq // kc, kc, Q_BLOCK), I32),
                        pltpu.VMEM((kc, H_A * Q_BLOCK), BF16),
                        pltpu.VMEM((LANES, H_A * Q_BLOCK), F32),
                        pltpu.VMEM((H_A, Q_BLOCK), F32)],
        compiler_params=pltpu.CompilerParams(dimension_semantics=("arbitrary", "arbitrary"),
                                             vmem_limit_bytes=VMEM_LIMIT_BYTES),
        name="dsa",
    )(r3(aq), r3(iq), r3(small), r3(kvi), tri, eye, wout_t)
    return out.reshape(batch * seq, d)


def _mix_out_kernel(x_ref, ya_ref, hm_ref, yc_ref, g12_ref, wm_ref, wo_ref, o_ref):
    d = x_ref.shape[1]
    ym = _dot(hm_ref[...], wm_ref[...])
    merged = (ya_ref[...].astype(F32) + g12_ref[:, 0:d].astype(F32) * ym
              + g12_ref[:, d:2 * d].astype(F32) * yc_ref[...].astype(F32))
    o_ref[...] = x_ref[...] + _dot(merged.astype(BF16), wo_ref[...])


def _mix_out(x, ya, hm, yc, g12, wm, wo):
    t, d = x.shape
    tm = min(TM_FFN, t)
    tok = lambda i: (i, 0)
    return pl.pallas_call(
        _mix_out_kernel,
        grid=(t // tm,),
        in_specs=[pl.BlockSpec((tm, d), tok), pl.BlockSpec((tm, d), tok), pl.BlockSpec((tm, D_MV), tok),
                  pl.BlockSpec((tm, d), tok), pl.BlockSpec((tm, 2 * d), tok), _const_spec(wm.shape),
                  _const_spec(wo.shape)],
        out_specs=pl.BlockSpec((tm, d), tok),
        out_shape=jax.ShapeDtypeStruct((t, d), F32),
        compiler_params=pltpu.CompilerParams(dimension_semantics=("arbitrary",),
                                             vmem_limit_bytes=VMEM_LIMIT_BYTES),
        name="mix_out",
    )(x, ya, hm, yc, g12, wm, wo)


def _pad_heads(w, heads, dh):
    d = w.shape[0]
    return jnp.pad(w.reshape(d, heads, dh), ((0, 0), (0, 0), (0, LANES - dh))).reshape(d, heads * LANES)


def _mixer_weights(w_in):
    d = w_in.shape[0]
    wb = w_in.astype(BF16)
    z = lambda n: jnp.zeros((d, n), BF16)
    col = lambda o, n: wb[:, o:o + n]
    return {
        "conv": col(O_CB, 3 * D_CONV),
        "mqk": col(O_MQ, 2 * H_M * DQK_M),
        "mv": col(O_MV, D_MV),
        "mo": col(O_MO, D_MV),
        "small": jnp.concatenate([col(O_MI, 2 * H_M), col(O_IW, H_IDX), z(LANES - 2 * H_M - H_IDX)], axis=1),
        "aq": _pad_heads(col(O_AQ, H_A * DH_A) * jnp.asarray(DH_A ** -0.5, BF16), H_A, DH_A),
        "kvi": jnp.concatenate([col(O_AK, DH_A), z(LANES - DH_A), col(O_AV, DH_A), z(LANES - DH_A),
                                col(O_IK, D_IDX), z(LANES - D_IDX)], axis=1),
        "iq": _pad_heads(col(O_IQ, H_IDX * D_IDX), H_IDX, D_IDX),
        "gates": col(O_GATES, 3 * d),
    }


def kernel(x, p, norm_ffn1, ffn1_w_gu, ffn1_w_down, norm_mix, w_in, conv_w, conv_w_out, mlstm_b_i, mlstm_b_f,
           mlstm_norm, mlstm_w_out, attn_w_out, w_o, norm_ffn2, ffn2_w_gu, ffn2_w_down, norm_ple, ple_w_gate,
           ple_w_proj, final_norm):
    batch, seq, d = x.shape
    depth = w_in.shape[0]
    t = batch * seq
    xs = x.reshape(t, d)
    for l in range(depth):
        xs = _ffn(xs, norm_ffn1[l], ffn1_w_gu[l].astype(BF16), ffn1_w_down[l].astype(BF16))
        pr = _mix_in(xs, norm_mix[l], _mixer_weights(w_in[l]), conv_w[l], conv_w_out[l].astype(BF16), seq)
        bias_row = jnp.concatenate([mlstm_b_i[l], mlstm_b_f[l], jnp.zeros((LANES - 2 * H_M,), F32)]).reshape(1, LANES)
        hm = _mlstm(pr["mqk"], pr["mv"], pr["smo"], pr["small"], bias_row, mlstm_norm[l], batch, seq)
        wout_t = jnp.pad(attn_w_out[l].astype(BF16).reshape(H_A, DH_A, d).transpose(0, 2, 1),
                         ((0, 0), (0, 0), (0, LANES - DH_A)))
        yc = _dsa(pr["aq"], pr["iq"], pr["small"], pr["kvi"], wout_t, batch, seq)
        xs = _mix_out(xs, pr["ya"], hm, yc, pr["g12"], mlstm_w_out[l].astype(BF16), w_o[l].astype(BF16))
        xs = _ffn(xs, norm_ffn2[l], ffn2_w_gu[l].astype(BF16), ffn2_w_down[l].astype(BF16),
                  ple=(p[l].reshape(t, p.shape[-1]), norm_ple[l], ple_w_gate[l].astype(BF16),
                       ple_w_proj[l].astype(BF16)),
                  final_g=final_norm if l == depth - 1 else None)
    return xs.reshape(batch, seq, d)
```

```python
import functools

import jax
import jax.numpy as jnp
from jax import lax
from jax.experimental import pallas as pl
from jax.experimental.pallas import tpu as pltpu

BF16 = jnp.bfloat16
F32 = jnp.float32
I32 = jnp.int32
I16 = jnp.int16

D_PLE = 256
D_FF = 2816
D_CONV = 512
CONV_W = 3
H_M = 4
DQK_M = 128
DV_M = 256
D_MV = H_M * DV_M
H_A = 8
DH_A = 64
H_IDX = 4
D_IDX = 64
TOPK_MAX = 256
Q_BLOCK = 128
EPS = 1e-6

_SPLITS = (D_CONV, D_CONV, D_CONV, H_M * DQK_M, H_M * DQK_M, D_MV, D_MV, H_M, H_M,
           H_A * DH_A, DH_A, DH_A, H_IDX * D_IDX, D_IDX, H_IDX)
_OFF = [0]
for _s in _SPLITS:
    _OFF.append(_OFF[-1] + _s)
(O_CB, O_CC, O_CX, O_MQ, O_MK, O_MV, O_MO, O_MI, O_MF, O_AQ, O_AK, O_AV, O_IQ, O_IK, O_IW, O_GATES) = _OFF

LANES = 128
VMEM_LIMIT_BYTES = 56 * 1024 * 1024

TM_FFN = 512
TM_MIX = 512
FF_CHUNK = 512
MLSTM_L = 256
DSA_KC = 512
INT_MIN = -2 ** 31
MASK_BIAS = -1e30
M_FLOOR = -1e29
LOG2E = 1.4426950408889634
ACC_ROWS = 64
HALF16 = 2 ** 15
BF16_EXACT_INT = 256


def _const_spec(shape):
    nd = len(shape)
    return pl.BlockSpec(shape, lambda *_: (0,) * nd, pipeline_mode=pl.Buffered(1))


def _rms(x, g):
    return x * lax.rsqrt(jnp.mean(x * x, axis=-1, keepdims=True) + EPS) * g


def _dot(a, b):
    return jnp.dot(a, b, preferred_element_type=F32)


def _dot_nt(a, b):
    return lax.dot_general(a, b, (((1,), (1,)), ((), ())), preferred_element_type=F32)


def _ffn_kernel(*refs, chunks, d_ff, with_ple, with_final):
    if with_ple:
        x_ref, g_ref, wgu_ref, wd_ref, p_ref, gp_ref, wpg_ref, wpp_ref = refs[:8]
        rest = refs[8:]
    else:
        x_ref, g_ref, wgu_ref, wd_ref = refs[:4]
        rest = refs[4:]
    if with_final:
        gf_ref, o_ref = rest
    else:
        (o_ref,) = rest
    x = x_ref[...]
    h = _rms(x, g_ref[...]).astype(BF16)
    acc = jnp.zeros_like(x)
    for off, w in chunks:
        gg = _dot(h, wgu_ref[:, off:off + w])
        uu = _dot(h, wgu_ref[:, d_ff + off:d_ff + off + w])
        a = (gg * jax.nn.sigmoid(gg) * uu).astype(BF16)
        acc = acc + _dot(a, wd_ref[off:off + w, :])
    x = x + 0.5 * acc
    if with_ple:
        gate = jax.nn.sigmoid(_dot(_rms(x, gp_ref[...]).astype(BF16), wpg_ref[...]))
        x = x + gate * _dot(p_ref[...].astype(BF16), wpp_ref[...])
    if with_final:
        x = _rms(x, gf_ref[...])
    o_ref[...] = x


def _ffn(x, g, wgu, wd, ple=None, final_g=None):
    t, d = x.shape
    d_ff = wd.shape[0]
    tm = min(TM_FFN, t)
    chunks = []
    off = 0
    while off < d_ff:
        w = min(FF_CHUNK, d_ff - off)
        chunks.append((off, w))
        off += w
    tok = lambda i: (i, 0)
    in_specs = [pl.BlockSpec((tm, d), tok), _const_spec((1, d)), _const_spec(wgu.shape), _const_spec(wd.shape)]
    args = [x, g.reshape(1, d), wgu, wd]
    if ple is not None:
        p, layer, gp, wpg, wpp = ple
        in_specs += [pl.BlockSpec((None, tm, p.shape[2]), lambda i: (layer, i, 0)), _const_spec((1, d)),
                     _const_spec(wpg.shape), _const_spec(wpp.shape)]
        args += [p, gp.reshape(1, d), wpg, wpp]
    if final_g is not None:
        in_specs.append(_const_spec((1, d)))
        args.append(final_g.reshape(1, d))
    kern = functools.partial(_ffn_kernel, chunks=tuple(chunks), d_ff=d_ff, with_ple=ple is not None,
                             with_final=final_g is not None)
    return pl.pallas_call(
        kern,
        grid=(t // tm,),
        in_specs=in_specs,
        out_specs=pl.BlockSpec((tm, d), tok),
        out_shape=jax.ShapeDtypeStruct((t, d), F32),
        compiler_params=pltpu.CompilerParams(dimension_semantics=("arbitrary",),
                                             vmem_limit_bytes=VMEM_LIMIT_BYTES),
        name="ffn",
    )(*args)


def _mix_in_kernel(x_ref, g_ref, wconv_ref, wmqk_ref, wmv_ref, wmo_ref, wsmall_ref, waq_ref, wkvi_ref, wiq_ref,
                   wgate_ref, cw_ref, cwo_ref,
                   ya_ref, mqk_ref, mv_ref, smo_ref, small_ref, aq_ref, kvi_ref, iq_ref, g12_ref,
                   ubuf_ref, *, tiles_per_seq):
    tm, d = x_ref.shape
    h = _rms(x_ref[...], g_ref[...]).astype(BF16)

    mqk_ref[...] = _dot(h, wmqk_ref[...]).astype(BF16)
    mv_ref[...] = _dot(h, wmv_ref[...]).astype(BF16)
    smo_ref[...] = jax.nn.sigmoid(_dot(h, wmo_ref[...])).astype(BF16)
    small_ref[...] = _dot(h, wsmall_ref[...])
    aq_ref[...] = (_dot(h, waq_ref[...]) * LOG2E).astype(BF16)
    kvi_ref[...] = _dot(h, wkvi_ref[...]).astype(BF16)
    iq_ref[...] = _dot(h, wiq_ref[...]).astype(BF16)
    g12_ref[...] = jax.nn.sigmoid(_dot(h, wgate_ref[:, d:3 * d])).astype(BF16)

    @pl.when(pl.program_id(0) % tiles_per_seq == 0)
    def _():
        ubuf_ref[0:8, :] = jnp.zeros((8, D_CONV), F32)

    cb = _dot(h, wconv_ref[:, 0:D_CONV])
    u = _dot(h, wconv_ref[:, D_CONV:2 * D_CONV]) * _dot(h, wconv_ref[:, 2 * D_CONV:3 * D_CONV])
    ubuf_ref[8:tm + 8, :] = u
    y = (ubuf_ref[6:tm + 6, :] * cw_ref[0:1, :] + ubuf_ref[7:tm + 7, :] * cw_ref[1:2, :] + u * cw_ref[2:3, :])
    ubuf_ref[0:8, :] = ubuf_ref[tm:tm + 8, :]
    ya = _dot((cb * y).astype(BF16), cwo_ref[...])
    ya_ref[...] = (jax.nn.sigmoid(_dot(h, wgate_ref[:, 0:d])) * ya).astype(BF16)


def _mix_in(x, g, w, conv_w, conv_w_out, seq):
    t, d = x.shape
    tm = min(TM_MIX, seq)
    tok = lambda i: (i, 0)
    weights = [w["conv"], w["mqk"], w["mv"], w["mo"], w["small"], w["aq"], w["kvi"], w["iq"], w["gates"],
               conv_w, conv_w_out]
    outs = [("ya", d, BF16), ("mqk", 2 * H_M * DQK_M, BF16), ("mv", D_MV, BF16), ("smo", D_MV, BF16),
            ("small", LANES, F32), ("aq", H_A * LANES, BF16), ("kvi", 3 * LANES, BF16), ("iq", H_IDX * LANES, BF16),
            ("g12", 2 * d, BF16)]
    res = pl.pallas_call(
        functools.partial(_mix_in_kernel, tiles_per_seq=seq // tm),
        grid=(t // tm,),
        in_specs=[pl.BlockSpec((tm, d), tok), _const_spec((1, d))] + [_const_spec(a.shape) for a in weights],
        out_specs=[pl.BlockSpec((tm, n), tok) for _, n, _ in outs],
        out_shape=[jax.ShapeDtypeStruct((t, n), dt) for _, n, dt in outs],
        scratch_shapes=[pltpu.VMEM((tm + 8, D_CONV), F32)],
        compiler_params=pltpu.CompilerParams(dimension_semantics=("arbitrary",),
                                             vmem_limit_bytes=VMEM_LIMIT_BYTES),
        name="mix_in",
    )(x, g.reshape(1, d), *weights)
    return {name: r for (name, _, _), r in zip(outs, res)}


def _log_sigmoid(x):
    return jnp.minimum(x, 0.0) - jnp.log(1.0 + jnp.exp(-jnp.abs(x)))


def _mlstm_kernel(qk_ref, v_ref, smo_ref, small_ref, bias_ref, nw_ref, tri_ref, o_ref, ct_ref, n_ref, m_ref):
    L = qk_ref.shape[0]
    scale = DQK_M ** -0.5

    @pl.when(pl.program_id(1) == 0)
    def _():
        ct_ref[...] = jnp.zeros(ct_ref.shape, F32)
        n_ref[...] = jnp.zeros(n_ref.shape, F32)
        m_ref[...] = jnp.zeros(m_ref.shape, F32)

    sm = small_ref[...] + bias_ref[...]
    lane = lax.broadcasted_iota(I32, sm.shape, 1)
    g = jnp.where((lane >= H_M) & (lane < 2 * H_M), _log_sigmoid(sm), sm)
    tri = tri_ref[...]
    g1 = g.astype(BF16)
    r1 = g - g1.astype(F32)
    g2 = r1.astype(BF16)
    g3 = (r1 - g2.astype(F32)).astype(BF16)
    cs = _dot(tri, g1) + _dot(tri, g2) + _dot(tri, g3)
    g_t = g.T
    cs_t = cs.T
    row = lax.broadcasted_iota(I32, (L, L), 0)
    col = lax.broadcasted_iota(I32, (L, L), 1)
    causal = col <= row

    for h in range(H_M):
        i_c = g[:, h:h + 1]
        b_c = cs[:, H_M + h:H_M + h + 1]
        a_r = g_t[h:h + 1, :] - cs_t[H_M + h:H_M + h + 1, :]
        m_prev = m_ref[h:h + 1, 0:1]
        dlog = jnp.where(causal, b_c + a_r, -jnp.inf)
        inter = b_c + m_prev
        m_t = jnp.maximum(inter, jnp.max(dlog, axis=1, keepdims=True))
        dm = jnp.exp(dlog - m_t)
        q = qk_ref[:, h * DQK_M:(h + 1) * DQK_M]
        k = qk_ref[:, (H_M + h) * DQK_M:(H_M + h + 1) * DQK_M]
        v = v_ref[:, h * DV_M:(h + 1) * DV_M]
        sc = _dot_nt(q, k) * scale * dm
        w_int = jnp.exp(inter - m_t)
        ct = ct_ref[h]
        num = w_int * _dot(q, ct.astype(BF16)) + _dot(sc.astype(BF16), v)
        n_row = n_ref[h:h + 1, :]
        qn = jnp.sum(q.astype(F32) * n_row, axis=1, keepdims=True)
        den = w_int * qn + jnp.sum(sc, axis=1, keepdims=True)
        hv = num / jnp.maximum(jnp.abs(den), jnp.exp(-m_t))
        hn = hv * lax.rsqrt(jnp.mean(hv * hv, axis=-1, keepdims=True) + EPS) * nw_ref[:, h * DV_M:(h + 1) * DV_M]
        o_ref[:, h * DV_M:(h + 1) * DV_M] = (smo_ref[:, h * DV_M:(h + 1) * DV_M].astype(F32) * hn).astype(BF16)
        b_last = cs[L - 1:L, H_M + h:H_M + h + 1]
        g_c = b_last - b_c + i_c
        m_new = jnp.maximum(b_last + m_prev, jnp.max(g_c, axis=0, keepdims=True))
        wk = jnp.exp(g_c - m_new) * scale
        decay = jnp.exp(b_last + m_prev - m_new)
        kw = k.astype(F32) * wk
        ct_ref[h] = decay * ct + _dot(kw.T.astype(BF16), v)
        n_ref[h:h + 1, :] = decay * n_row + jnp.sum(kw, axis=0, keepdims=True)
        m_ref[h:h + 1, :] = jnp.broadcast_to(m_new, (1, LANES))


def _mlstm(mqk, mv, smo, small, bias_row, norm_w, batch, seq):
    L = min(MLSTM_L, seq)
    r3 = lambda a: a.reshape(batch, seq, a.shape[-1])
    blk = lambda n: pl.BlockSpec((None, L, n), lambda b, c: (b, c, 0))
    tri = jnp.tril(jnp.ones((L, L), BF16))
    out = pl.pallas_call(
        _mlstm_kernel,
        grid=(batch, seq // L),
        in_specs=[blk(2 * H_M * DQK_M), blk(D_MV), blk(D_MV), blk(LANES), _const_spec((1, LANES)),
                  _const_spec((1, D_MV)), _const_spec((L, L))],
        out_specs=blk(D_MV),
        out_shape=jax.ShapeDtypeStruct((batch, seq, D_MV), BF16),
        scratch_shapes=[pltpu.VMEM((H_M, DQK_M, DV_M), F32), pltpu.VMEM((8, LANES), F32),
                        pltpu.VMEM((8, LANES), F32)],
        compiler_params=pltpu.CompilerParams(dimension_semantics=("arbitrary", "arbitrary"),
                                             vmem_limit_bytes=VMEM_LIMIT_BYTES),
        name="mlstm",
    )(r3(mqk), r3(mv), r3(smo), r3(small), bias_row, norm_w.reshape(1, D_MV), tri)
    return out.reshape(batch * seq, D_MV)


def _dsa_kernel(aq_ref, iq_ref, small_ref, kvi_ref, tri_ref, eye_ref, o_ref,
                keys_ref, khi_ref, klo_ref, klo2_ref, s0_ref, s1_ref, p_ref, acc_ref, m_ref, *, n_sel):
    kc = keys_ref.shape[1]
    qb = pl.program_id(1)
    nkc = (qb * Q_BLOCK) // kc + 1
    t_idx = qb * Q_BLOCK + lax.broadcasted_iota(I32, (kc, Q_BLOCK), 1)
    s_iota = lax.broadcasted_iota(I32, (kc, Q_BLOCK), 0)

    def part(x):
        return x.reshape(kc // ACC_ROWS, ACC_ROWS, Q_BLOCK)

    def rows_sum(x):
        return jnp.sum(jnp.sum(part(x), axis=0), axis=0, keepdims=True)

    iq_all = jnp.concatenate([iq_ref[:, h * LANES:(h + 1) * LANES] for h in range(H_IDX)], axis=0)
    w_scale = (H_IDX ** -0.5) * (D_IDX ** -0.5)
    sm_t = small_ref[...].T
    w_rows = [sm_t[2 * H_M + h:2 * H_M + h + 1, :] * w_scale for h in range(H_IDX)]

    hk = kc // 2
    vis_margin = (t_idx - s_iota)[0:hk]

    def score_body(c, carry):
        for sub in range(2):
            s0 = c * kc + sub * hk
            rows = slice(sub * hk, (sub + 1) * hk)
            ik_c = kvi_ref[pl.ds(pl.multiple_of(s0, hk), hk), 2 * LANES:3 * LANES]
            lg = _dot_nt(ik_c, iq_all)
            sc = w_rows[0] * jnp.maximum(lg[:, 0:Q_BLOCK], 0.0)
            for h in range(1, H_IDX):
                sc = sc + w_rows[h] * jnp.maximum(lg[:, h * Q_BLOCK:(h + 1) * Q_BLOCK], 0.0)
            bits = lax.bitcast_convert_type(sc, I32)
            bits = jnp.where(bits == INT_MIN, 0, bits)
            key = bits ^ ((bits >> 31) & 0x7FFFFFFF)
            key = jnp.where(vis_margin >= s0, key, INT_MIN)
            keys_ref[c, rows, :] = key
            khi_ref[c, rows, :] = (key >> 16).astype(I16)
            klo_ref[c, rows, :] = ((key & 0xFFFF) - HALF16).astype(I16)
        return carry

    lax.fori_loop(0, nkc, score_body, 0)

    one_b = jnp.ones((kc, Q_BLOCK), BF16)
    zero_b = jnp.zeros((kc, Q_BLOCK), BF16)

    def count16(pred):
        def body(c, acc):
            parts = jnp.where(pred(c), one_b, zero_b)
            parts = [parts[j * ACC_ROWS:(j + 1) * ACC_ROWS] for j in range(kc // ACC_ROWS)]
            while len(parts) > 1:
                parts = [parts[j] + parts[j + 1] for j in range(0, len(parts), 2)]
            return acc + parts[0]
        acc = lax.fori_loop(0, nkc, body, jnp.zeros((ACC_ROWS, Q_BLOCK), BF16))
        return jnp.sum(acc.astype(F32), axis=0, keepdims=True)

    def radix16(ref, n_take):
        def bit_body(i, t_acc):
            cand = t_acc | jnp.left_shift(jnp.int32(1), 15 - i)
            c16 = (cand - HALF16).astype(I16)
            return jnp.where(count16(lambda c: ref[c] >= c16) >= n_take, cand, t_acc)
        return lax.fori_loop(0, 16, bit_body, jnp.zeros((1, Q_BLOCK), I32))

    thr_hi = radix16(khi_ref, n_sel) - HALF16
    thr_hi16 = thr_hi.astype(I16)

    def above_hi(c):
        hi = khi_ref[c]
        klo2_ref[c] = jnp.where(hi == thr_hi16, klo_ref[c], jnp.int16(-HALF16))
        return hi > thr_hi16

    n_gt_hi = count16(above_hi)
    t_lo = radix16(klo2_ref, n_sel - n_gt_hi)
    t_lo16 = (t_lo - HALF16).astype(I16)
    thr = thr_hi * (2 * HALF16) + t_lo
    n_gt = n_gt_hi + count16(lambda c: klo2_ref[c] > t_lo16)
    n_eq = count16(lambda c: (khi_ref[c] == thr_hi16) & (klo_ref[c] == t_lo16))
    need = n_sel - n_gt
    excess_ties = jnp.max(n_eq - need) > 0.0

    q_all = jnp.concatenate([aq_ref[:, h * LANES:(h + 1) * LANES] for h in range(H_A)], axis=0)
    q_aug = jnp.concatenate([q_all, eye_ref[...]], axis=1)
    m_ref[...] = jnp.full(m_ref.shape, M_FLOOR, F32)
    acc_ref[...] = jnp.zeros(acc_ref.shape, F32)
    d_iota = lax.broadcasted_iota(I32, (LANES, kc), 0)

    def logits(c, s_ref, tie_carry, with_ties):
        start = pl.multiple_of(c * kc, kc)
        k = keys_ref[c]
        if with_ties:
            eq = k == thr
            eq_f = jnp.where(eq, 1.0, 0.0)
            rank = _dot(tri_ref[...], eq_f.astype(BF16)) + tie_carry
            sel = ((k > thr) | (eq & (rank < need))) & (k > INT_MIN)
            tie_carry = tie_carry + rows_sum(eq_f)
        else:
            sel = (k >= thr) & (k > INT_MIN)
        bias = jnp.where(sel, 0.0, MASK_BIAS).astype(BF16)
        k_aug = jnp.concatenate([kvi_ref[pl.ds(start, kc), 0:LANES], bias], axis=1)
        s_ref[...] = _dot_nt(k_aug, q_aug)
        return tie_carry

    def accumulate(c, s_ref):
        start = pl.multiple_of(c * kc, kc)
        v_c = kvi_ref[pl.ds(start, kc), LANES:2 * LANES]
        v_t = jnp.where(d_iota == DH_A, 1.0, v_c.astype(F32).T).astype(BF16)
        for h in range(H_A):
            cols = slice(h * Q_BLOCK, (h + 1) * Q_BLOCK)
            sh = s_ref[:, cols]
            m_old = m_ref[h:h + 1, :]
            m_new = jnp.maximum(m_old, jnp.max(jnp.max(part(sh), axis=0), axis=0, keepdims=True))
            p_ref[:, cols] = jnp.exp2(sh - m_new).astype(BF16)
            m_ref[h:h + 1, :] = m_new
            acc_ref[:, cols] = acc_ref[:, cols] * jnp.exp2(m_old - m_new)
        acc_ref[...] = acc_ref[...] + _dot(v_t, p_ref[...])

    def attend(with_ties):
        last = nkc - 1
        tie = logits(0, s0_ref, jnp.zeros((1, Q_BLOCK), F32), with_ties)

        def pair(j, tie):
            c = 2 * j
            tie = logits(c + 1, s1_ref, tie, with_ties)
            accumulate(c, s0_ref)
            tie = logits(jnp.minimum(c + 2, last), s0_ref, tie, with_ties)
            accumulate(c + 1, s1_ref)
            return tie

        lax.fori_loop(0, nkc // 2, pair, tie)

        @pl.when(nkc % 2 == 1)
        def _():
            accumulate(last, s0_ref)

    @pl.when(excess_ties)
    def _():
        attend(True)

    @pl.when(jnp.logical_not(excess_ties))
    def _():
        attend(False)

    lane = lax.broadcasted_iota(I32, (Q_BLOCK, LANES), 1)
    pairs = []
    for hp in range(H_A // 2):
        o = []
        for h in (2 * hp, 2 * hp + 1):
            cols = slice(h * Q_BLOCK, (h + 1) * Q_BLOCK)
            o.append((acc_ref[:, cols] / acc_ref[DH_A:DH_A + 1, cols]).T)
        pairs.append(jnp.where(lane < DH_A, o[0], pltpu.roll(o[1], DH_A, axis=1)))
    o_ref[...] = jnp.concatenate(pairs, axis=1).astype(BF16)


def _dsa(aq, iq, small, kvi, batch, seq):
    d = H_A * DH_A
    kc = min(DSA_KC, seq)
    assert (kc // ACC_ROWS) * (seq // kc) <= BF16_EXACT_INT
    n_sel = min(TOPK_MAX, seq // 4)
    r3 = lambda a: a.reshape(batch, seq, a.shape[-1])
    qblk = lambda n: pl.BlockSpec((None, Q_BLOCK, n), lambda b, q: (b, q, 0))
    tri = jnp.tril(jnp.ones((kc, kc), BF16), k=-1)
    eye = jnp.tile(jnp.eye(Q_BLOCK, dtype=BF16), (H_A, 1))
    out = pl.pallas_call(
        functools.partial(_dsa_kernel, n_sel=n_sel),
        grid=(batch, seq // Q_BLOCK),
        in_specs=[qblk(H_A * LANES), qblk(H_IDX * LANES), qblk(LANES),
                  pl.BlockSpec((None, seq, 3 * LANES), lambda b, q: (b, 0, 0)),
                  _const_spec((kc, kc)), _const_spec(eye.shape)],
        out_specs=qblk(d),
        out_shape=jax.ShapeDtypeStruct((batch, seq, d), BF16),
        scratch_shapes=[pltpu.VMEM((seq // kc, kc, Q_BLOCK), I32),
                        pltpu.VMEM((seq // kc, kc, Q_BLOCK), I16),
                        pltpu.VMEM((seq // kc, kc, Q_BLOCK), I16),
                        pltpu.VMEM((seq // kc, kc, Q_BLOCK), I16),
                        pltpu.VMEM((kc, H_A * Q_BLOCK), F32),
                        pltpu.VMEM((kc, H_A * Q_BLOCK), F32),
                        pltpu.VMEM((kc, H_A * Q_BLOCK), BF16),
                        pltpu.VMEM((LANES, H_A * Q_BLOCK), F32),
                        pltpu.VMEM((H_A, Q_BLOCK), F32)],
        compiler_params=pltpu.CompilerParams(dimension_semantics=("arbitrary", "arbitrary"),
                                             vmem_limit_bytes=VMEM_LIMIT_BYTES),
        name="dsa",
    )(r3(aq), r3(iq), r3(small), r3(kvi), tri, eye)
    return out.reshape(batch * seq, d)


def _mix_out_kernel(x_ref, ya_ref, hm_ref, ha_ref, g12_ref, wm_ref, wa_ref, wo_ref, o_ref):
    d = x_ref.shape[1]
    ym = _dot(hm_ref[...], wm_ref[...])
    yc = _dot(ha_ref[...], wa_ref[...])
    merged = ya_ref[...].astype(F32) + g12_ref[:, 0:d].astype(F32) * ym + g12_ref[:, d:2 * d].astype(F32) * yc
    o_ref[...] = x_ref[...] + _dot(merged.astype(BF16), wo_ref[...])


def _mix_out(x, ya, hm, ha, g12, wm, wa, wo):
    t, d = x.shape
    tm = min(TM_FFN, t)
    tok = lambda i: (i, 0)
    return pl.pallas_call(
        _mix_out_kernel,
        grid=(t // tm,),
        in_specs=[pl.BlockSpec((tm, d), tok), pl.BlockSpec((tm, d), tok), pl.BlockSpec((tm, D_MV), tok),
                  pl.BlockSpec((tm, ha.shape[1]), tok), pl.BlockSpec((tm, 2 * d), tok), _const_spec(wm.shape),
                  _const_spec(wa.shape), _const_spec(wo.shape)],
        out_specs=pl.BlockSpec((tm, d), tok),
        out_shape=jax.ShapeDtypeStruct((t, d), F32),
        compiler_params=pltpu.CompilerParams(dimension_semantics=("arbitrary",),
                                             vmem_limit_bytes=VMEM_LIMIT_BYTES),
        name="mix_out",
    )(x, ya, hm, ha, g12, wm, wa, wo)


def _pad_heads(w, heads, dh):
    d = w.shape[0]
    return jnp.pad(w.reshape(d, heads, dh), ((0, 0), (0, 0), (0, LANES - dh))).reshape(d, heads * LANES)


def _mixer_weights(w_in):
    d = w_in.shape[0]
    wb = w_in.astype(BF16)
    z = lambda n: jnp.zeros((d, n), BF16)
    col = lambda o, n: wb[:, o:o + n]
    return {
        "conv": col(O_CB, 3 * D_CONV),
        "mqk": col(O_MQ, 2 * H_M * DQK_M),
        "mv": col(O_MV, D_MV),
        "mo": col(O_MO, D_MV),
        "small": jnp.concatenate([col(O_MI, 2 * H_M), col(O_IW, H_IDX), z(LANES - 2 * H_M - H_IDX)], axis=1),
        "aq": _pad_heads(col(O_AQ, H_A * DH_A) * jnp.asarray(DH_A ** -0.5, BF16), H_A, DH_A),
        "kvi": jnp.concatenate([col(O_AK, DH_A), z(LANES - DH_A), col(O_AV, DH_A), z(LANES - DH_A),
                                col(O_IK, D_IDX), z(LANES - D_IDX)], axis=1),
        "iq": _pad_heads(col(O_IQ, H_IDX * D_IDX), H_IDX, D_IDX),
        "gates": col(O_GATES, 3 * d),
    }


def kernel(x, p, norm_ffn1, ffn1_w_gu, ffn1_w_down, norm_mix, w_in, conv_w, conv_w_out, mlstm_b_i, mlstm_b_f,
           mlstm_norm, mlstm_w_out, attn_w_out, w_o, norm_ffn2, ffn2_w_gu, ffn2_w_down, norm_ple, ple_w_gate,
           ple_w_proj, final_norm):
    batch, seq, d = x.shape
    depth = w_in.shape[0]
    t = batch * seq
    xs = x.reshape(t, d)
    for l in range(depth):
        xs = _ffn(xs, norm_ffn1[l], ffn1_w_gu[l].astype(BF16), ffn1_w_down[l].astype(BF16))
        pr = _mix_in(xs, norm_mix[l], _mixer_weights(w_in[l]), conv_w[l], conv_w_out[l].astype(BF16), seq)
        bias_row = jnp.concatenate([mlstm_b_i[l], mlstm_b_f[l], jnp.zeros((LANES - 2 * H_M,), F32)]).reshape(1, LANES)
        hm = _mlstm(pr["mqk"], pr["mv"], pr["smo"], pr["small"], bias_row, mlstm_norm[l], batch, seq)
        ha = _dsa(pr["aq"], pr["iq"], pr["small"], pr["kvi"], batch, seq)
        xs = _mix_out(xs, pr["ya"], hm, ha, pr["g12"], mlstm_w_out[l].astype(BF16), attn_w_out[l].astype(BF16),
                      w_o[l].astype(BF16))
        xs = _ffn(xs, norm_ffn2[l], ffn2_w_gu[l].astype(BF16), ffn2_w_down[l].astype(BF16),
                  ple=(p.reshape(depth, t, p.shape[-1]), l, norm_ple[l], ple_w_gate[l].astype(BF16),
                       ple_w_proj[l].astype(BF16)),
                  final_g=final_norm if l == depth - 1 else None)
    return xs.reshape(batch, seq, d)
```

```python
import functools

import jax
import jax.numpy as jnp
from jax import lax
from jax.experimental import pallas as pl
from jax.experimental.pallas import tpu as pltpu

BF16 = jnp.bfloat16
F32 = jnp.float32
I32 = jnp.int32

D_PLE = 256
D_FF = 2816
D_CONV = 512
CONV_W = 3
H_M = 4
DQK_M = 128
DV_M = 256
D_MV = H_M * DV_M
H_A = 8
DH_A = 64
H_IDX = 4
D_IDX = 64
TOPK_MAX = 256
Q_BLOCK = 128
EPS = 1e-6

_SPLITS = (D_CONV, D_CONV, D_CONV, H_M * DQK_M, H_M * DQK_M, D_MV, D_MV, H_M, H_M,
           H_A * DH_A, DH_A, DH_A, H_IDX * D_IDX, D_IDX, H_IDX)
_OFF = [0]
for _s in _SPLITS:
    _OFF.append(_OFF[-1] + _s)
(O_CB, O_CC, O_CX, O_MQ, O_MK, O_MV, O_MO, O_MI, O_MF, O_AQ, O_AK, O_AV, O_IQ, O_IK, O_IW, O_GATES) = _OFF

LANES = 128
VMEM_LIMIT_BYTES = 56 * 1024 * 1024

TM_FFN = 1024
TM_MIX = 512
FF_CHUNK = 512
MLSTM_L = 256
DSA_KC = 512
INT_MIN = -2 ** 31
MASK_BIAS = -1e30
M_FLOOR = -1e29
LOG2E = 1.4426950408889634
ACC_ROWS = 64

def _const_spec(shape):
    nd = len(shape)
    return pl.BlockSpec(shape, lambda *_: (0,) * nd, pipeline_mode=pl.Buffered(1))


def _rms(x, g):
    return x * lax.rsqrt(jnp.mean(x * x, axis=-1, keepdims=True) + EPS) * g


def _dot(a, b):
    return jnp.dot(a, b, preferred_element_type=F32)


def _dot_nt(a, b):
    return lax.dot_general(a, b, (((1,), (1,)), ((), ())), preferred_element_type=F32)


def _ffn_kernel(*refs, chunks, d_ff, with_ple, with_final):
    if with_ple:
        x_ref, g_ref, wgu_ref, wd_ref, p_ref, gp_ref, wpg_ref, wpp_ref = refs[:8]
        rest = refs[8:]
    else:
        x_ref, g_ref, wgu_ref, wd_ref = refs[:4]
        rest = refs[4:]
    if with_final:
        gf_ref, o_ref = rest
    else:
        (o_ref,) = rest
    x = x_ref[...]
    h = _rms(x, g_ref[...]).astype(BF16)
    acc = jnp.zeros_like(x)
    for off, w in chunks:
        gg = _dot(h, wgu_ref[:, off:off + w])
        uu = _dot(h, wgu_ref[:, d_ff + off:d_ff + off + w])
        a = (gg * jax.nn.sigmoid(gg) * uu).astype(BF16)
        acc = acc + _dot(a, wd_ref[off:off + w, :])
    x = x + 0.5 * acc
    if with_ple:
        gate = jax.nn.sigmoid(_dot(_rms(x, gp_ref[...]).astype(BF16), wpg_ref[...]))
        x = x + gate * _dot(p_ref[...].astype(BF16), wpp_ref[...])
    if with_final:
        x = _rms(x, gf_ref[...])
    o_ref[...] = x


def _ffn(x, g, wgu, wd, ple=None, final_g=None):
    t, d = x.shape
    d_ff = wd.shape[0]
    tm = min(TM_FFN, t)
    chunks = []
    off = 0
    while off < d_ff:
        w = min(FF_CHUNK, d_ff - off)
        chunks.append((off, w))
        off += w
    tok = lambda i: (i, 0)
    in_specs = [pl.BlockSpec((tm, d), tok), _const_spec((1, d)), _const_spec(wgu.shape), _const_spec(wd.shape)]
    args = [x, g.reshape(1, d), wgu, wd]
    if ple is not None:
        p, layer, gp, wpg, wpp = ple
        in_specs += [pl.BlockSpec((None, tm, p.shape[2]), lambda i: (layer, i, 0)), _const_spec((1, d)),
                     _const_spec(wpg.shape), _const_spec(wpp.shape)]
        args += [p, gp.reshape(1, d), wpg, wpp]
    if final_g is not None:
        in_specs.append(_const_spec((1, d)))
        args.append(final_g.reshape(1, d))
    kern = functools.partial(_ffn_kernel, chunks=tuple(chunks), d_ff=d_ff, with_ple=ple is not None,
                             with_final=final_g is not None)
    return pl.pallas_call(
        kern,
        grid=(t // tm,),
        in_specs=in_specs,
        out_specs=pl.BlockSpec((tm, d), tok),
        out_shape=jax.ShapeDtypeStruct((t, d), F32),
        compiler_params=pltpu.CompilerParams(dimension_semantics=("arbitrary",),
                                             vmem_limit_bytes=VMEM_LIMIT_BYTES),
        name="ffn",
    )(*args)


def _mix_in_kernel(x_ref, g_ref, wconv_ref, wmqk_ref, wmv_ref, wmo_ref, wsmall_ref, waq_ref, wkvi_ref, wiq_ref,
                   wgate_ref, cw_ref, cwo_ref,
                   ya_ref, mqk_ref, mv_ref, smo_ref, small_ref, aq_ref, kvi_ref, iq_ref, g12_ref,
                   ubuf_ref, *, tiles_per_seq):
    tm, d = x_ref.shape
    h = _rms(x_ref[...], g_ref[...]).astype(BF16)

    mqk_ref[...] = _dot(h, wmqk_ref[...]).astype(BF16)
    mv_ref[...] = _dot(h, wmv_ref[...]).astype(BF16)
    smo_ref[...] = jax.nn.sigmoid(_dot(h, wmo_ref[...])).astype(BF16)
    small_ref[...] = _dot(h, wsmall_ref[...])
    aq_ref[...] = (_dot(h, waq_ref[...]) * LOG2E).astype(BF16)
    kvi_ref[...] = _dot(h, wkvi_ref[...]).astype(BF16)
    iq_ref[...] = _dot(h, wiq_ref[...]).astype(BF16)
    g12_ref[...] = jax.nn.sigmoid(_dot(h, wgate_ref[:, d:3 * d])).astype(BF16)

    @pl.when(pl.program_id(0) % tiles_per_seq == 0)
    def _():
        ubuf_ref[0:8, :] = jnp.zeros((8, D_CONV), F32)

    cb = _dot(h, wconv_ref[:, 0:D_CONV])
    u = _dot(h, wconv_ref[:, D_CONV:2 * D_CONV]) * _dot(h, wconv_ref[:, 2 * D_CONV:3 * D_CONV])
    ubuf_ref[8:tm + 8, :] = u
    y = (ubuf_ref[6:tm + 6, :] * cw_ref[0:1, :] + ubuf_ref[7:tm + 7, :] * cw_ref[1:2, :] + u * cw_ref[2:3, :])
    ubuf_ref[0:8, :] = ubuf_ref[tm:tm + 8, :]
    ya = _dot((cb * y).astype(BF16), cwo_ref[...])
    ya_ref[...] = (jax.nn.sigmoid(_dot(h, wgate_ref[:, 0:d])) * ya).astype(BF16)


def _mix_in(x, g, w, conv_w, conv_w_out, seq):
    t, d = x.shape
    tm = min(TM_MIX, seq)
    tok = lambda i: (i, 0)
    weights = [w["conv"], w["mqk"], w["mv"], w["mo"], w["small"], w["aq"], w["kvi"], w["iq"], w["gates"],
               conv_w, conv_w_out]
    outs = [("ya", d, BF16), ("mqk", 2 * H_M * DQK_M, BF16), ("mv", D_MV, BF16), ("smo", D_MV, BF16),
            ("small", LANES, F32), ("aq", H_A * LANES, BF16), ("kvi", 3 * LANES, BF16), ("iq", H_IDX * LANES, BF16),
            ("g12", 2 * d, BF16)]
    res = pl.pallas_call(
        functools.partial(_mix_in_kernel, tiles_per_seq=seq // tm),
        grid=(t // tm,),
        in_specs=[pl.BlockSpec((tm, d), tok), _const_spec((1, d))] + [_const_spec(a.shape) for a in weights],
        out_specs=[pl.BlockSpec((tm, n), tok) for _, n, _ in outs],
        out_shape=[jax.ShapeDtypeStruct((t, n), dt) for _, n, dt in outs],
        scratch_shapes=[pltpu.VMEM((tm + 8, D_CONV), F32)],
        compiler_params=pltpu.CompilerParams(dimension_semantics=("arbitrary",),
                                             vmem_limit_bytes=VMEM_LIMIT_BYTES),
        name="mix_in",
    )(x, g.reshape(1, d), *weights)
    return {name: r for (name, _, _), r in zip(outs, res)}


def _log_sigmoid(x):
    return jnp.minimum(x, 0.0) - jnp.log(1.0 + jnp.exp(-jnp.abs(x)))


def _mlstm_kernel(qk_ref, v_ref, smo_ref, small_ref, bias_ref, nw_ref, tri_ref, o_ref, ct_ref, n_ref, m_ref):
    L = qk_ref.shape[0]
    scale = DQK_M ** -0.5

    @pl.when(pl.program_id(1) == 0)
    def _():
        ct_ref[...] = jnp.zeros(ct_ref.shape, F32)
        n_ref[...] = jnp.zeros(n_ref.shape, F32)
        m_ref[...] = jnp.zeros(m_ref.shape, F32)

    sm = small_ref[...] + bias_ref[...]
    lane = lax.broadcasted_iota(I32, sm.shape, 1)
    g = jnp.where((lane >= H_M) & (lane < 2 * H_M), _log_sigmoid(sm), sm)
    tri = tri_ref[...]
    g1 = g.astype(BF16)
    r1 = g - g1.astype(F32)
    g2 = r1.astype(BF16)
    g3 = (r1 - g2.astype(F32)).astype(BF16)
    cs = _dot(tri, g1) + _dot(tri, g2) + _dot(tri, g3)
    g_t = g.T
    cs_t = cs.T
    row = lax.broadcasted_iota(I32, (L, L), 0)
    col = lax.broadcasted_iota(I32, (L, L), 1)
    causal = col <= row

    for h in range(H_M):
        i_c = g[:, h:h + 1]
        b_c = cs[:, H_M + h:H_M + h + 1]
        a_r = g_t[h:h + 1, :] - cs_t[H_M + h:H_M + h + 1, :]
        m_prev = m_ref[h:h + 1, 0:1]
        dlog = jnp.where(causal, b_c + a_r, -jnp.inf)
        inter = b_c + m_prev
        m_t = jnp.maximum(inter, jnp.max(dlog, axis=1, keepdims=True))
        dm = jnp.exp(dlog - m_t)
        q = qk_ref[:, h * DQK_M:(h + 1) * DQK_M]
        k = qk_ref[:, (H_M + h) * DQK_M:(H_M + h + 1) * DQK_M]
        v = v_ref[:, h * DV_M:(h + 1) * DV_M]
        sc = _dot_nt(q, k) * scale * dm
        w_int = jnp.exp(inter - m_t)
        ct = ct_ref[h]
        num = w_int * _dot(q, ct.astype(BF16)) + _dot(sc.astype(BF16), v)
        n_row = n_ref[h:h + 1, :]
        qn = jnp.sum(q.astype(F32) * n_row, axis=1, keepdims=True)
        den = w_int * qn + jnp.sum(sc, axis=1, keepdims=True)
        hv = num / jnp.maximum(jnp.abs(den), jnp.exp(-m_t))
        hn = hv * lax.rsqrt(jnp.mean(hv * hv, axis=-1, keepdims=True) + EPS) * nw_ref[:, h * DV_M:(h + 1) * DV_M]
        o_ref[:, h * DV_M:(h + 1) * DV_M] = (smo_ref[:, h * DV_M:(h + 1) * DV_M].astype(F32) * hn).astype(BF16)
        b_last = cs[L - 1:L, H_M + h:H_M + h + 1]
        g_c = b_last - b_c + i_c
        m_new = jnp.maximum(b_last + m_prev, jnp.max(g_c, axis=0, keepdims=True))
        wk = jnp.exp(g_c - m_new) * scale
        decay = jnp.exp(b_last + m_prev - m_new)
        kw = k.astype(F32) * wk
        ct_ref[h] = decay * ct + _dot(kw.T.astype(BF16), v)
        n_ref[h:h + 1, :] = decay * n_row + jnp.sum(kw, axis=0, keepdims=True)
        m_ref[h:h + 1, :] = jnp.broadcast_to(m_new, (1, LANES))


def _mlstm(mqk, mv, smo, small, bias_row, norm_w, batch, seq):
    L = min(MLSTM_L, seq)
    r3 = lambda a: a.reshape(batch, seq, a.shape[-1])
    blk = lambda n: pl.BlockSpec((None, L, n), lambda b, c: (b, c, 0))
    tri = jnp.tril(jnp.ones((L, L), BF16))
    out = pl.pallas_call(
        _mlstm_kernel,
        grid=(batch, seq // L),
        in_specs=[blk(2 * H_M * DQK_M), blk(D_MV), blk(D_MV), blk(LANES), _const_spec((1, LANES)),
                  _const_spec((1, D_MV)), _const_spec((L, L))],
        out_specs=blk(D_MV),
        out_shape=jax.ShapeDtypeStruct((batch, seq, D_MV), BF16),
        scratch_shapes=[pltpu.VMEM((H_M, DQK_M, DV_M), F32), pltpu.VMEM((8, LANES), F32),
                        pltpu.VMEM((8, LANES), F32)],
        compiler_params=pltpu.CompilerParams(dimension_semantics=("arbitrary", "arbitrary"),
                                             vmem_limit_bytes=VMEM_LIMIT_BYTES),
        name="mlstm",
    )(r3(mqk), r3(mv), r3(smo), r3(small), bias_row, norm_w.reshape(1, D_MV), tri)
    return out.reshape(batch * seq, D_MV)


def _dsa_kernel(aq_ref, iq_ref, small_ref, kvi_ref, tri_ref, eye_ref, o_ref,
                keys_ref, s0_ref, s1_ref, p_ref, acc_ref, m_ref, *, n_sel):
    kc = keys_ref.shape[1]
    qb = pl.program_id(1)
    nkc = (qb * Q_BLOCK) // kc + 1
    t_idx = qb * Q_BLOCK + lax.broadcasted_iota(I32, (kc, Q_BLOCK), 1)
    s_iota = lax.broadcasted_iota(I32, (kc, Q_BLOCK), 0)

    def part(x):
        return x.reshape(kc // ACC_ROWS, ACC_ROWS, Q_BLOCK)

    def rows_sum(x):
        return jnp.sum(jnp.sum(part(x), axis=0), axis=0, keepdims=True)

    iq_all = jnp.concatenate([iq_ref[:, h * LANES:(h + 1) * LANES] for h in range(H_IDX)], axis=0)
    w_scale = (H_IDX ** -0.5) * (D_IDX ** -0.5)
    sm_t = small_ref[...].T
    w_rows = [sm_t[2 * H_M + h:2 * H_M + h + 1, :] * w_scale for h in range(H_IDX)]

    hk = kc // 2
    vis_margin = (t_idx - s_iota)[0:hk]

    def score_body(c, carry):
        for sub in range(2):
            s0 = c * kc + sub * hk
            rows = slice(sub * hk, (sub + 1) * hk)
            ik_c = kvi_ref[pl.ds(pl.multiple_of(s0, hk), hk), 2 * LANES:3 * LANES]
            lg = _dot_nt(ik_c, iq_all)
            sc = w_rows[0] * jnp.maximum(lg[:, 0:Q_BLOCK], 0.0)
            for h in range(1, H_IDX):
                sc = sc + w_rows[h] * jnp.maximum(lg[:, h * Q_BLOCK:(h + 1) * Q_BLOCK], 0.0)
            bits = lax.bitcast_convert_type(sc, I32)
            bits = jnp.where(bits == INT_MIN, 0, bits)
            key = bits ^ ((bits >> 31) & 0x7FFFFFFF)
            key = jnp.where(vis_margin >= s0, key, INT_MIN)
            keys_ref[c, rows, :] = key
        return carry

    lax.fori_loop(0, nkc, score_body, 0)

    def count_keys(pred):
        def body(c, acc):
            return acc + jnp.sum(part(jnp.where(pred(keys_ref[c]), 1.0, 0.0)), axis=0)
        acc = lax.fori_loop(0, nkc, body, jnp.zeros((ACC_ROWS, Q_BLOCK), F32))
        return jnp.sum(acc, axis=0, keepdims=True)

    def bit_body(i, t_acc):
        cand = t_acc | jnp.left_shift(jnp.int32(1), 31 - i)
        cmp_row = cand ^ INT_MIN
        return jnp.where(count_keys(lambda k: k >= cmp_row) >= n_sel, cand, t_acc)

    thr = lax.fori_loop(0, 32, bit_body, jnp.zeros((1, Q_BLOCK), I32)) ^ INT_MIN
    n_gt = count_keys(lambda k: k > thr)
    n_eq = count_keys(lambda k: k == thr)
    need = n_sel - n_gt
    excess_ties = jnp.max(n_eq - need) > 0.0

    q_all = jnp.concatenate([aq_ref[:, h * LANES:(h + 1) * LANES] for h in range(H_A)], axis=0)
    q_aug = jnp.concatenate([q_all, eye_ref[...]], axis=1)
    m_ref[...] = jnp.full(m_ref.shape, M_FLOOR, F32)
    acc_ref[...] = jnp.zeros(acc_ref.shape, F32)
    d_iota = lax.broadcasted_iota(I32, (LANES, kc), 0)

    def logits(c, s_ref, tie_carry, with_ties):
        start = pl.multiple_of(c * kc, kc)
        k = keys_ref[c]
        if with_ties:
            eq = k == thr
            eq_f = jnp.where(eq, 1.0, 0.0)
            rank = _dot(tri_ref[...], eq_f.astype(BF16)) + tie_carry
            sel = ((k > thr) | (eq & (rank < need))) & (k > INT_MIN)
            tie_carry = tie_carry + rows_sum(eq_f)
        else:
            sel = (k >= thr) & (k > INT_MIN)
        bias = jnp.where(sel, 0.0, MASK_BIAS).astype(BF16)
        k_aug = jnp.concatenate([kvi_ref[pl.ds(start, kc), 0:LANES], bias], axis=1)
        s_ref[:, 0:H_A * Q_BLOCK] = _dot_nt(k_aug, q_aug)
        return tie_carry

    def accumulate(c, s_ref):
        start = pl.multiple_of(c * kc, kc)
        v_c = kvi_ref[pl.ds(start, kc), LANES:2 * LANES]
        v_t = jnp.where(d_iota == DH_A, 1.0, v_c.astype(F32).T).astype(BF16)
        for h in range(H_A):
            cols = slice(h * Q_BLOCK, (h + 1) * Q_BLOCK)
            sh = s_ref[:, cols]
            m_old = m_ref[h:h + 1, :]
            m_new = jnp.maximum(m_old, jnp.max(jnp.max(part(sh), axis=0), axis=0, keepdims=True))
            p_ref[:, cols] = jnp.exp2(sh - m_new).astype(BF16)
            m_ref[h:h + 1, :] = m_new
            acc_ref[:, cols] = acc_ref[:, cols] * jnp.exp2(m_old - m_new)
        acc_ref[:, 0:H_A * Q_BLOCK] = acc_ref[:, 0:H_A * Q_BLOCK] + _dot(v_t, p_ref[:, 0:H_A * Q_BLOCK])

    def attend(with_ties):
        last = nkc - 1
        tie = logits(0, s0_ref, jnp.zeros((1, Q_BLOCK), F32), with_ties)

        def pair(j, tie):
            c = 2 * j
            tie = logits(c + 1, s1_ref, tie, with_ties)
            accumulate(c, s0_ref)
            tie = logits(jnp.minimum(c + 2, last), s0_ref, tie, with_ties)
            accumulate(c + 1, s1_ref)
            return tie

        lax.fori_loop(0, nkc // 2, pair, tie)

        @pl.when(nkc % 2 == 1)
        def _():
            accumulate(last, s0_ref)

    @pl.when(excess_ties)
    def _():
        attend(True)

    @pl.when(jnp.logical_not(excess_ties))
    def _():
        attend(False)

    lane = lax.broadcasted_iota(I32, (Q_BLOCK, LANES), 1)
    pairs = []
    for hp in range(H_A // 2):
        o = []
        for h in (2 * hp, 2 * hp + 1):
            cols = slice(h * Q_BLOCK, (h + 1) * Q_BLOCK)
            o.append((acc_ref[:, cols] / acc_ref[DH_A:DH_A + 1, cols]).T)
        pairs.append(jnp.where(lane < DH_A, o[0], pltpu.roll(o[1], DH_A, axis=1)))
    o_ref[...] = jnp.concatenate(pairs, axis=1).astype(BF16)


def _dsa(aq, iq, small, kvi, batch, seq):
    d = H_A * DH_A
    kc = min(DSA_KC, seq)
    n_sel = min(TOPK_MAX, seq // 4)
    r3 = lambda a: a.reshape(batch, seq, a.shape[-1])
    qblk = lambda n: pl.BlockSpec((None, Q_BLOCK, n), lambda b, q: (b, q, 0))
    tri = jnp.tril(jnp.ones((kc, kc), BF16), k=-1)
    eye = jnp.tile(jnp.eye(Q_BLOCK, dtype=BF16), (H_A, 1))
    out = pl.pallas_call(
        functools.partial(_dsa_kernel, n_sel=n_sel),
        grid=(batch, seq // Q_BLOCK),
        in_specs=[qblk(H_A * LANES), qblk(H_IDX * LANES), qblk(LANES),
                  pl.BlockSpec((None, seq, 3 * LANES), lambda b, q: (b, 0, 0)),
                  _const_spec((kc, kc)), _const_spec(eye.shape)],
        out_specs=qblk(d),
        out_shape=jax.ShapeDtypeStruct((batch, seq, d), BF16),
        scratch_shapes=[pltpu.VMEM((seq // kc, kc, Q_BLOCK), I32),
                        pltpu.VMEM((kc, H_A * Q_BLOCK + LANES), F32),
                        pltpu.VMEM((kc, H_A * Q_BLOCK + LANES), F32),
                        pltpu.VMEM((kc, H_A * Q_BLOCK + LANES), BF16),
                        pltpu.VMEM((LANES, H_A * Q_BLOCK + LANES), F32),
                        pltpu.VMEM((H_A, Q_BLOCK), F32)],
        compiler_params=pltpu.CompilerParams(dimension_semantics=("arbitrary", "arbitrary"),
                                             vmem_limit_bytes=VMEM_LIMIT_BYTES),
        name="dsa",
    )(r3(aq), r3(iq), r3(small), r3(kvi), tri, eye)
    return out.reshape(batch * seq, d)


def _mix_out_kernel(x_ref, ya_ref, hm_ref, ha_ref, g12_ref, wm_ref, wa_ref, wo_ref, o_ref):
    d = x_ref.shape[1]
    ym = _dot(hm_ref[...], wm_ref[...])
    yc = _dot(ha_ref[...], wa_ref[...])
    merged = ya_ref[...].astype(F32) + g12_ref[:, 0:d].astype(F32) * ym + g12_ref[:, d:2 * d].astype(F32) * yc
    o_ref[...] = x_ref[...] + _dot(merged.astype(BF16), wo_ref[...])


def _mix_out(x, ya, hm, ha, g12, wm, wa, wo):
    t, d = x.shape
    tm = min(TM_FFN, t)
    tok = lambda i: (i, 0)
    return pl.pallas_call(
        _mix_out_kernel,
        grid=(t // tm,),
        in_specs=[pl.BlockSpec((tm, d), tok), pl.BlockSpec((tm, d), tok), pl.BlockSpec((tm, D_MV), tok),
                  pl.BlockSpec((tm, ha.shape[1]), tok), pl.BlockSpec((tm, 2 * d), tok), _const_spec(wm.shape),
                  _const_spec(wa.shape), _const_spec(wo.shape)],
        out_specs=pl.BlockSpec((tm, d), tok),
        out_shape=jax.ShapeDtypeStruct((t, d), F32),
        compiler_params=pltpu.CompilerParams(dimension_semantics=("arbitrary",),
                                             vmem_limit_bytes=VMEM_LIMIT_BYTES),
        name="mix_out",
    )(x, ya, hm, ha, g12, wm, wa, wo)


def _pad_heads(w, heads, dh):
    d = w.shape[0]
    return jnp.pad(w.reshape(d, heads, dh), ((0, 0), (0, 0), (0, LANES - dh))).reshape(d, heads * LANES)


def _mixer_weights(w_in):
    d = w_in.shape[0]
    wb = w_in.astype(BF16)
    z = lambda n: jnp.zeros((d, n), BF16)
    col = lambda o, n: wb[:, o:o + n]
    return {
        "conv": col(O_CB, 3 * D_CONV),
        "mqk": col(O_MQ, 2 * H_M * DQK_M),
        "mv": col(O_MV, D_MV),
        "mo": col(O_MO, D_MV),
        "small": jnp.concatenate([col(O_MI, 2 * H_M), col(O_IW, H_IDX), z(LANES - 2 * H_M - H_IDX)], axis=1),
        "aq": _pad_heads(col(O_AQ, H_A * DH_A) * jnp.asarray(DH_A ** -0.5, BF16), H_A, DH_A),
        "kvi": jnp.concatenate([col(O_AK, DH_A), z(LANES - DH_A), col(O_AV, DH_A), z(LANES - DH_A),
                                col(O_IK, D_IDX), z(LANES - D_IDX)], axis=1),
        "iq": _pad_heads(col(O_IQ, H_IDX * D_IDX), H_IDX, D_IDX),
        "gates": col(O_GATES, 3 * d),
    }


def kernel(x, p, norm_ffn1, ffn1_w_gu, ffn1_w_down, norm_mix, w_in, conv_w, conv_w_out, mlstm_b_i, mlstm_b_f,
           mlstm_norm, mlstm_w_out, attn_w_out, w_o, norm_ffn2, ffn2_w_gu, ffn2_w_down, norm_ple, ple_w_gate,
           ple_w_proj, final_norm):
    batch, seq, d = x.shape
    depth = w_in.shape[0]
    t = batch * seq
    xs = x.reshape(t, d)
    for l in range(depth):
        xs = _ffn(xs, norm_ffn1[l], ffn1_w_gu[l].astype(BF16), ffn1_w_down[l].astype(BF16))
        pr = _mix_in(xs, norm_mix[l], _mixer_weights(w_in[l]), conv_w[l], conv_w_out[l].astype(BF16), seq)
        bias_row = jnp.concatenate([mlstm_b_i[l], mlstm_b_f[l], jnp.zeros((LANES - 2 * H_M,), F32)]).reshape(1, LANES)
        hm = _mlstm(pr["mqk"], pr["mv"], pr["smo"], pr["small"], bias_row, mlstm_norm[l], batch, seq)
        ha = _dsa(pr["aq"], pr["iq"], pr["small"], pr["kvi"], batch, seq)
        xs = _mix_out(xs, pr["ya"], hm, ha, pr["g12"], mlstm_w_out[l].astype(BF16), attn_w_out[l].astype(BF16),
                      w_o[l].astype(BF16))
        xs = _ffn(xs, norm_ffn2[l], ffn2_w_gu[l].astype(BF16), ffn2_w_down[l].astype(BF16),
                  ple=(p.reshape(depth, t, p.shape[-1]), l, norm_ple[l], ple_w_gate[l].astype(BF16),
                       ple_w_proj[l].astype(BF16)),
                  final_g=final_norm if l == depth - 1 else None)
    return xs.reshape(batch, seq, d)
```

```python
import functools

import jax
import jax.numpy as jnp
from jax import lax
from jax.experimental import pallas as pl
from jax.experimental.pallas import tpu as pltpu

BF16 = jnp.bfloat16
F32 = jnp.float32
I32 = jnp.int32

D_PLE = 256
D_FF = 2816
D_CONV = 512
CONV_W = 3
H_M = 4
DQK_M = 128
DV_M = 256
D_MV = H_M * DV_M
H_A = 8
DH_A = 64
H_IDX = 4
D_IDX = 64
TOPK_MAX = 256
Q_BLOCK = 128
EPS = 1e-6

_SPLITS = (D_CONV, D_CONV, D_CONV, H_M * DQK_M, H_M * DQK_M, D_MV, D_MV, H_M, H_M,
           H_A * DH_A, DH_A, DH_A, H_IDX * D_IDX, D_IDX, H_IDX)
_OFF = [0]
for _s in _SPLITS:
    _OFF.append(_OFF[-1] + _s)
(O_CB, O_CC, O_CX, O_MQ, O_MK, O_MV, O_MO, O_MI, O_MF, O_AQ, O_AK, O_AV, O_IQ, O_IK, O_IW, O_GATES) = _OFF

LANES = 128
VMEM_LIMIT_BYTES = 56 * 1024 * 1024

TM_FFN = 1024
TM_MIX = 512
FF_CHUNK = 512
MLSTM_L = 256
DSA_KC = 512
INT_MIN = -2 ** 31
MASK_BIAS = -1e30
M_FLOOR = -1e29
LOG2E = 1.4426950408889634
ACC_ROWS = 64
TRANSPOSE_MASKS = (0x0000FFFF, 0x00FF00FF, 0x0F0F0F0F, 0x33333333, 0x55555555)

def _const_spec(shape):
    nd = len(shape)
    return pl.BlockSpec(shape, lambda *_: (0,) * nd, pipeline_mode=pl.Buffered(1))


def _rms(x, g):
    return x * lax.rsqrt(jnp.mean(x * x, axis=-1, keepdims=True) + EPS) * g


def _dot(a, b):
    return jnp.dot(a, b, preferred_element_type=F32)


def _dot_nt(a, b):
    return lax.dot_general(a, b, (((1,), (1,)), ((), ())), preferred_element_type=F32)


def _ffn_kernel(*refs, chunks, d_ff, with_ple, with_final):
    if with_ple:
        x_ref, g_ref, wgu_ref, wd_ref, p_ref, gp_ref, wpg_ref, wpp_ref = refs[:8]
        rest = refs[8:]
    else:
        x_ref, g_ref, wgu_ref, wd_ref = refs[:4]
        rest = refs[4:]
    if with_final:
        gf_ref, o_ref = rest
    else:
        (o_ref,) = rest
    x = x_ref[...]
    h = _rms(x, g_ref[...]).astype(BF16)
    acc = jnp.zeros_like(x)
    for off, w in chunks:
        gg = _dot(h, wgu_ref[:, off:off + w])
        uu = _dot(h, wgu_ref[:, d_ff + off:d_ff + off + w])
        a = (gg * jax.nn.sigmoid(gg) * uu).astype(BF16)
        acc = acc + _dot(a, wd_ref[off:off + w, :])
    x = x + 0.5 * acc
    if with_ple:
        gate = jax.nn.sigmoid(_dot(_rms(x, gp_ref[...]).astype(BF16), wpg_ref[...]))
        x = x + gate * _dot(p_ref[...].astype(BF16), wpp_ref[...])
    if with_final:
        x = _rms(x, gf_ref[...])
    o_ref[...] = x


def _ffn(x, g, wgu, wd, ple=None, final_g=None):
    t, d = x.shape
    d_ff = wd.shape[0]
    tm = min(TM_FFN, t)
    chunks = []
    off = 0
    while off < d_ff:
        w = min(FF_CHUNK, d_ff - off)
        chunks.append((off, w))
        off += w
    tok = lambda i: (i, 0)
    in_specs = [pl.BlockSpec((tm, d), tok), _const_spec((1, d)), _const_spec(wgu.shape), _const_spec(wd.shape)]
    args = [x, g.reshape(1, d), wgu, wd]
    if ple is not None:
        p, layer, gp, wpg, wpp = ple
        in_specs += [pl.BlockSpec((None, tm, p.shape[2]), lambda i: (layer, i, 0)), _const_spec((1, d)),
                     _const_spec(wpg.shape), _const_spec(wpp.shape)]
        args += [p, gp.reshape(1, d), wpg, wpp]
    if final_g is not None:
        in_specs.append(_const_spec((1, d)))
        args.append(final_g.reshape(1, d))
    kern = functools.partial(_ffn_kernel, chunks=tuple(chunks), d_ff=d_ff, with_ple=ple is not None,
                             with_final=final_g is not None)
    return pl.pallas_call(
        kern,
        grid=(t // tm,),
        in_specs=in_specs,
        out_specs=pl.BlockSpec((tm, d), tok),
        out_shape=jax.ShapeDtypeStruct((t, d), F32),
        compiler_params=pltpu.CompilerParams(dimension_semantics=("arbitrary",),
                                             vmem_limit_bytes=VMEM_LIMIT_BYTES),
        name="ffn",
    )(*args)


def _mix_in_kernel(x_ref, g_ref, wconv_ref, wmqk_ref, wmv_ref, wmo_ref, wsmall_ref, waq_ref, wkvi_ref, wiq_ref,
                   wgate_ref, cw_ref, cwo_ref,
                   ya_ref, mqk_ref, mv_ref, smo_ref, small_ref, aq_ref, kvi_ref, iq_ref, g12_ref,
                   ubuf_ref, *, tiles_per_seq):
    tm, d = x_ref.shape
    h = _rms(x_ref[...], g_ref[...]).astype(BF16)

    mqk_ref[...] = _dot(h, wmqk_ref[...]).astype(BF16)
    mv_ref[...] = _dot(h, wmv_ref[...]).astype(BF16)
    smo_ref[...] = jax.nn.sigmoid(_dot(h, wmo_ref[...])).astype(BF16)
    small_ref[...] = _dot(h, wsmall_ref[...])
    aq_ref[...] = (_dot(h, waq_ref[...]) * LOG2E).astype(BF16)
    kvi_ref[...] = _dot(h, wkvi_ref[...]).astype(BF16)
    iq_ref[...] = _dot(h, wiq_ref[...]).astype(BF16)
    g12_ref[...] = jax.nn.sigmoid(_dot(h, wgate_ref[:, d:3 * d])).astype(BF16)

    @pl.when(pl.program_id(0) % tiles_per_seq == 0)
    def _():
        ubuf_ref[0:8, :] = jnp.zeros((8, D_CONV), F32)

    cb = _dot(h, wconv_ref[:, 0:D_CONV])
    u = _dot(h, wconv_ref[:, D_CONV:2 * D_CONV]) * _dot(h, wconv_ref[:, 2 * D_CONV:3 * D_CONV])
    ubuf_ref[8:tm + 8, :] = u
    y = (ubuf_ref[6:tm + 6, :] * cw_ref[0:1, :] + ubuf_ref[7:tm + 7, :] * cw_ref[1:2, :] + u * cw_ref[2:3, :])
    ubuf_ref[0:8, :] = ubuf_ref[tm:tm + 8, :]
    ya = _dot((cb * y).astype(BF16), cwo_ref[...])
    ya_ref[...] = (jax.nn.sigmoid(_dot(h, wgate_ref[:, 0:d])) * ya).astype(BF16)


def _mix_in(x, g, w, conv_w, conv_w_out, seq):
    t, d = x.shape
    tm = min(TM_MIX, seq)
    tok = lambda i: (i, 0)
    weights = [w["conv"], w["mqk"], w["mv"], w["mo"], w["small"], w["aq"], w["kvi"], w["iq"], w["gates"],
               conv_w, conv_w_out]
    outs = [("ya", d, BF16), ("mqk", 2 * H_M * DQK_M, BF16), ("mv", D_MV, BF16), ("smo", D_MV, BF16),
            ("small", LANES, F32), ("aq", H_A * LANES, BF16), ("kvi", 3 * LANES, BF16), ("iq", H_IDX * LANES, BF16),
            ("g12", 2 * d, BF16)]
    res = pl.pallas_call(
        functools.partial(_mix_in_kernel, tiles_per_seq=seq // tm),
        grid=(t // tm,),
        in_specs=[pl.BlockSpec((tm, d), tok), _const_spec((1, d))] + [_const_spec(a.shape) for a in weights],
        out_specs=[pl.BlockSpec((tm, n), tok) for _, n, _ in outs],
        out_shape=[jax.ShapeDtypeStruct((t, n), dt) for _, n, dt in outs],
        scratch_shapes=[pltpu.VMEM((tm + 8, D_CONV), F32)],
        compiler_params=pltpu.CompilerParams(dimension_semantics=("arbitrary",),
                                             vmem_limit_bytes=VMEM_LIMIT_BYTES),
        name="mix_in",
    )(x, g.reshape(1, d), *weights)
    return {name: r for (name, _, _), r in zip(outs, res)}


def _log_sigmoid(x):
    return jnp.minimum(x, 0.0) - jnp.log(1.0 + jnp.exp(-jnp.abs(x)))


def _mlstm_kernel(qk_ref, v_ref, smo_ref, small_ref, bias_ref, nw_ref, tri_ref, o_ref, ct_ref, n_ref, m_ref):
    L = qk_ref.shape[0]
    scale = DQK_M ** -0.5

    @pl.when(pl.program_id(1) == 0)
    def _():
        ct_ref[...] = jnp.zeros(ct_ref.shape, F32)
        n_ref[...] = jnp.zeros(n_ref.shape, F32)
        m_ref[...] = jnp.zeros(m_ref.shape, F32)

    sm = small_ref[...] + bias_ref[...]
    lane = lax.broadcasted_iota(I32, sm.shape, 1)
    g = jnp.where((lane >= H_M) & (lane < 2 * H_M), _log_sigmoid(sm), sm)
    tri = tri_ref[...]
    g1 = g.astype(BF16)
    r1 = g - g1.astype(F32)
    g2 = r1.astype(BF16)
    g3 = (r1 - g2.astype(F32)).astype(BF16)
    cs = _dot(tri, g1) + _dot(tri, g2) + _dot(tri, g3)
    g_t = g.T
    cs_t = cs.T
    row = lax.broadcasted_iota(I32, (L, L), 0)
    col = lax.broadcasted_iota(I32, (L, L), 1)
    causal = col <= row

    for h in range(H_M):
        i_c = g[:, h:h + 1]
        b_c = cs[:, H_M + h:H_M + h + 1]
        a_r = g_t[h:h + 1, :] - cs_t[H_M + h:H_M + h + 1, :]
        m_prev = m_ref[h:h + 1, 0:1]
        dlog = jnp.where(causal, b_c + a_r, -jnp.inf)
        inter = b_c + m_prev
        m_t = jnp.maximum(inter, jnp.max(dlog, axis=1, keepdims=True))
        dm = jnp.exp(dlog - m_t)
        q = qk_ref[:, h * DQK_M:(h + 1) * DQK_M]
        k = qk_ref[:, (H_M + h) * DQK_M:(H_M + h + 1) * DQK_M]
        v = v_ref[:, h * DV_M:(h + 1) * DV_M]
        sc = _dot_nt(q, k) * scale * dm
        w_int = jnp.exp(inter - m_t)
        ct = ct_ref[h]
        num = w_int * _dot(q, ct.astype(BF16)) + _dot(sc.astype(BF16), v)
        n_row = n_ref[h:h + 1, :]
        qn = jnp.sum(q.astype(F32) * n_row, axis=1, keepdims=True)
        den = w_int * qn + jnp.sum(sc, axis=1, keepdims=True)
        hv = num / jnp.maximum(jnp.abs(den), jnp.exp(-m_t))
        hn = hv * lax.rsqrt(jnp.mean(hv * hv, axis=-1, keepdims=True) + EPS) * nw_ref[:, h * DV_M:(h + 1) * DV_M]
        o_ref[:, h * DV_M:(h + 1) * DV_M] = (smo_ref[:, h * DV_M:(h + 1) * DV_M].astype(F32) * hn).astype(BF16)
        b_last = cs[L - 1:L, H_M + h:H_M + h + 1]
        g_c = b_last - b_c + i_c
        m_new = jnp.maximum(b_last + m_prev, jnp.max(g_c, axis=0, keepdims=True))
        wk = jnp.exp(g_c - m_new) * scale
        decay = jnp.exp(b_last + m_prev - m_new)
        kw = k.astype(F32) * wk
        ct_ref[h] = decay * ct + _dot(kw.T.astype(BF16), v)
        n_ref[h:h + 1, :] = decay * n_row + jnp.sum(kw, axis=0, keepdims=True)
        m_ref[h:h + 1, :] = jnp.broadcast_to(m_new, (1, LANES))


def _mlstm(mqk, mv, smo, small, bias_row, norm_w, batch, seq):
    L = min(MLSTM_L, seq)
    r3 = lambda a: a.reshape(batch, seq, a.shape[-1])
    blk = lambda n: pl.BlockSpec((None, L, n), lambda b, c: (b, c, 0))
    tri = jnp.tril(jnp.ones((L, L), BF16))
    out = pl.pallas_call(
        _mlstm_kernel,
        grid=(batch, seq // L),
        in_specs=[blk(2 * H_M * DQK_M), blk(D_MV), blk(D_MV), blk(LANES), _const_spec((1, LANES)),
                  _const_spec((1, D_MV)), _const_spec((L, L))],
        out_specs=blk(D_MV),
        out_shape=jax.ShapeDtypeStruct((batch, seq, D_MV), BF16),
        scratch_shapes=[pltpu.VMEM((H_M, DQK_M, DV_M), F32), pltpu.VMEM((8, LANES), F32),
                        pltpu.VMEM((8, LANES), F32)],
        compiler_params=pltpu.CompilerParams(dimension_semantics=("arbitrary", "arbitrary"),
                                             vmem_limit_bytes=VMEM_LIMIT_BYTES),
        name="mlstm",
    )(r3(mqk), r3(mv), r3(smo), r3(small), bias_row, norm_w.reshape(1, D_MV), tri)
    return out.reshape(batch * seq, D_MV)


def _dsa_kernel(aq_ref, iq_ref, small_ref, kvi_ref, tri_ref, eye_ref, o_ref,
                keys_ref, planes_ref, und_ref, s0_ref, s1_ref, p_ref, acc_ref, m_ref, *, n_sel):
    kc = keys_ref.shape[1]
    qb = pl.program_id(1)
    nkc = (qb * Q_BLOCK) // kc + 1
    t_idx = qb * Q_BLOCK + lax.broadcasted_iota(I32, (kc, Q_BLOCK), 1)
    s_iota = lax.broadcasted_iota(I32, (kc, Q_BLOCK), 0)

    def part(x):
        return x.reshape(kc // ACC_ROWS, ACC_ROWS, Q_BLOCK)

    def rows_sum(x):
        return jnp.sum(jnp.sum(part(x), axis=0), axis=0, keepdims=True)

    iq_all = jnp.concatenate([iq_ref[:, h * LANES:(h + 1) * LANES] for h in range(H_IDX)], axis=0)
    w_scale = (H_IDX ** -0.5) * (D_IDX ** -0.5)
    sm_t = small_ref[...].T
    w_rows = [sm_t[2 * H_M + h:2 * H_M + h + 1, :] * w_scale for h in range(H_IDX)]

    hk = kc // 2
    vis_margin = (t_idx - s_iota)[0:hk]

    def score_body(c, carry):
        for sub in range(2):
            s0 = c * kc + sub * hk
            rows = slice(sub * hk, (sub + 1) * hk)
            ik_c = kvi_ref[pl.ds(pl.multiple_of(s0, hk), hk), 2 * LANES:3 * LANES]
            lg = _dot_nt(ik_c, iq_all)
            sc = w_rows[0] * jnp.maximum(lg[:, 0:Q_BLOCK], 0.0)
            for h in range(1, H_IDX):
                sc = sc + w_rows[h] * jnp.maximum(lg[:, h * Q_BLOCK:(h + 1) * Q_BLOCK], 0.0)
            bits = lax.bitcast_convert_type(sc, I32)
            bits = jnp.where(bits == INT_MIN, 0, bits)
            key = bits ^ ((bits >> 31) & 0x7FFFFFFF)
            key = jnp.where(vis_margin >= s0, key, INT_MIN)
            keys_ref[c, rows, :] = key
            ukey = key ^ INT_MIN
            words = [ukey[8 * j:8 * (j + 1)] for j in range(32)]
            step = 16
            for mask in TRANSPOSE_MASKS:
                for j in range(32):
                    if j & step == 0:
                        hi = lax.shift_right_logical(words[j + step], jnp.full((8, Q_BLOCK), step, I32))
                        t = (words[j] ^ hi) & mask
                        words[j] = words[j] ^ t
                        words[j + step] = words[j + step] ^ jnp.left_shift(t, jnp.int32(step))
                step //= 2
            for i in range(32):
                planes_ref[c, i, 8 * sub:8 * (sub + 1), :] = words[i]
        und_ref[c] = jnp.full(und_ref.shape[1:], -1, I32)
        return carry

    lax.fori_loop(0, nkc, score_body, 0)

    def sweep(i_prev, flip, i_next):
        def body(c, acc):
            und = und_ref[c]
            if i_prev is not None:
                und = und & (planes_ref[c, i_prev] ^ flip)
                und_ref[c] = und
            if i_next is not None:
                und = und & planes_ref[c, i_next]
            return acc + lax.population_count(und)
        acc = lax.fori_loop(0, nkc, body, jnp.zeros(und_ref.shape[1:], I32))
        return jnp.sum(acc.astype(F32), axis=0, keepdims=True)

    def decide(i, n_set, t_bits, n_above):
        ok = n_above + n_set >= n_sel
        return (jnp.where(ok, t_bits | jnp.left_shift(jnp.int32(1), 31 - i), t_bits),
                jnp.where(ok, n_above, n_above + n_set), jnp.where(ok, 0, -1))

    def bit_body(i, state):
        t_bits, n_above, flip = state
        return decide(i, sweep(i - 1, flip, i), t_bits, n_above)

    state = decide(0, sweep(None, None, 0), jnp.zeros((1, Q_BLOCK), I32), jnp.zeros((1, Q_BLOCK), F32))
    t_bits, n_gt, flip = lax.fori_loop(1, 32, bit_body, state)
    n_eq = sweep(31, flip, None)
    thr = t_bits ^ INT_MIN
    need = n_sel - n_gt
    excess_ties = jnp.max(n_eq - need) > 0.0

    q_all = jnp.concatenate([aq_ref[:, h * LANES:(h + 1) * LANES] for h in range(H_A)], axis=0)
    q_aug = jnp.concatenate([q_all, eye_ref[...]], axis=1)
    m_ref[...] = jnp.full(m_ref.shape, M_FLOOR, F32)
    acc_ref[...] = jnp.zeros(acc_ref.shape, F32)
    d_iota = lax.broadcasted_iota(I32, (LANES, kc), 0)

    def logits(c, s_ref, tie_carry, with_ties):
        start = pl.multiple_of(c * kc, kc)
        k = keys_ref[c]
        if with_ties:
            eq = k == thr
            eq_f = jnp.where(eq, 1.0, 0.0)
            rank = _dot(tri_ref[...], eq_f.astype(BF16)) + tie_carry
            sel = ((k > thr) | (eq & (rank < need))) & (k > INT_MIN)
            tie_carry = tie_carry + rows_sum(eq_f)
        else:
            sel = (k >= thr) & (k > INT_MIN)
        bias = jnp.where(sel, 0.0, MASK_BIAS).astype(BF16)
        k_aug = jnp.concatenate([kvi_ref[pl.ds(start, kc), 0:LANES], bias], axis=1)
        s_ref[:, 0:H_A * Q_BLOCK] = _dot_nt(k_aug, q_aug)
        return tie_carry

    def accumulate(c, s_ref):
        start = pl.multiple_of(c * kc, kc)
        v_c = kvi_ref[pl.ds(start, kc), LANES:2 * LANES]
        v_t = jnp.where(d_iota == DH_A, 1.0, v_c.astype(F32).T).astype(BF16)
        for h in range(H_A):
            cols = slice(h * Q_BLOCK, (h + 1) * Q_BLOCK)
            sh = s_ref[:, cols]
            m_old = m_ref[h:h + 1, :]
            m_new = jnp.maximum(m_old, jnp.max(jnp.max(part(sh), axis=0), axis=0, keepdims=True))
            p_ref[:, cols] = jnp.exp2(sh - m_new).astype(BF16)
            m_ref[h:h + 1, :] = m_new
            acc_ref[:, cols] = acc_ref[:, cols] * jnp.exp2(m_old - m_new)
        acc_ref[:, 0:H_A * Q_BLOCK] = acc_ref[:, 0:H_A * Q_BLOCK] + _dot(v_t, p_ref[:, 0:H_A * Q_BLOCK])

    def attend(with_ties):
        last = nkc - 1
        tie = logits(0, s0_ref, jnp.zeros((1, Q_BLOCK), F32), with_ties)

        def pair(j, tie):
            c = 2 * j
            tie = logits(c + 1, s1_ref, tie, with_ties)
            accumulate(c, s0_ref)
            tie = logits(jnp.minimum(c + 2, last), s0_ref, tie, with_ties)
            accumulate(c + 1, s1_ref)
            return tie

        lax.fori_loop(0, nkc // 2, pair, tie)

        @pl.when(nkc % 2 == 1)
        def _():
            accumulate(last, s0_ref)

    @pl.when(excess_ties)
    def _():
        attend(True)

    @pl.when(jnp.logical_not(excess_ties))
    def _():
        attend(False)

    lane = lax.broadcasted_iota(I32, (Q_BLOCK, LANES), 1)
    pairs = []
    for hp in range(H_A // 2):
        o = []
        for h in (2 * hp, 2 * hp + 1):
            cols = slice(h * Q_BLOCK, (h + 1) * Q_BLOCK)
            o.append((acc_ref[:, cols] / acc_ref[DH_A:DH_A + 1, cols]).T)
        pairs.append(jnp.where(lane < DH_A, o[0], pltpu.roll(o[1], DH_A, axis=1)))
    o_ref[...] = jnp.concatenate(pairs, axis=1).astype(BF16)


def _dsa(aq, iq, small, kvi, batch, seq):
    d = H_A * DH_A
    kc = min(DSA_KC, seq)
    n_sel = min(TOPK_MAX, seq // 4)
    r3 = lambda a: a.reshape(batch, seq, a.shape[-1])
    qblk = lambda n: pl.BlockSpec((None, Q_BLOCK, n), lambda b, q: (b, q, 0))
    tri = jnp.tril(jnp.ones((kc, kc), BF16), k=-1)
    eye = jnp.tile(jnp.eye(Q_BLOCK, dtype=BF16), (H_A, 1))
    out = pl.pallas_call(
        functools.partial(_dsa_kernel, n_sel=n_sel),
        grid=(batch, seq // Q_BLOCK),
        in_specs=[qblk(H_A * LANES), qblk(H_IDX * LANES), qblk(LANES),
                  pl.BlockSpec((None, seq, 3 * LANES), lambda b, q: (b, 0, 0)),
                  _const_spec((kc, kc)), _const_spec(eye.shape)],
        out_specs=qblk(d),
        out_shape=jax.ShapeDtypeStruct((batch, seq, d), BF16),
        scratch_shapes=[pltpu.VMEM((seq // kc, kc, Q_BLOCK), I32),
                        pltpu.VMEM((seq // kc, 32, kc // 32, Q_BLOCK), I32),
                        pltpu.VMEM((seq // kc, kc // 32, Q_BLOCK), I32),
                        pltpu.VMEM((kc, H_A * Q_BLOCK + LANES), F32),
                        pltpu.VMEM((kc, H_A * Q_BLOCK + LANES), F32),
                        pltpu.VMEM((kc, H_A * Q_BLOCK + LANES), BF16),
                        pltpu.VMEM((LANES, H_A * Q_BLOCK + LANES), F32),
                        pltpu.VMEM((H_A, Q_BLOCK), F32)],
        compiler_params=pltpu.CompilerParams(dimension_semantics=("arbitrary", "arbitrary"),
                                             vmem_limit_bytes=VMEM_LIMIT_BYTES),
        name="dsa",
    )(r3(aq), r3(iq), r3(small), r3(kvi), tri, eye)
    return out.reshape(batch * seq, d)


def _mix_out_kernel(x_ref, ya_ref, hm_ref, ha_ref, g12_ref, wm_ref, wa_ref, wo_ref, o_ref):
    d = x_ref.shape[1]
    ym = _dot(hm_ref[...], wm_ref[...])
    yc = _dot(ha_ref[...], wa_ref[...])
    merged = ya_ref[...].astype(F32) + g12_ref[:, 0:d].astype(F32) * ym + g12_ref[:, d:2 * d].astype(F32) * yc
    o_ref[...] = x_ref[...] + _dot(merged.astype(BF16), wo_ref[...])


def _mix_out(x, ya, hm, ha, g12, wm, wa, wo):
    t, d = x.shape
    tm = min(TM_FFN, t)
    tok = lambda i: (i, 0)
    return pl.pallas_call(
        _mix_out_kernel,
        grid=(t // tm,),
        in_specs=[pl.BlockSpec((tm, d), tok), pl.BlockSpec((tm, d), tok), pl.BlockSpec((tm, D_MV), tok),
                  pl.BlockSpec((tm, ha.shape[1]), tok), pl.BlockSpec((tm, 2 * d), tok), _const_spec(wm.shape),
                  _const_spec(wa.shape), _const_spec(wo.shape)],
        out_specs=pl.BlockSpec((tm, d), tok),
        out_shape=jax.ShapeDtypeStruct((t, d), F32),
        compiler_params=pltpu.CompilerParams(dimension_semantics=("arbitrary",),
                                             vmem_limit_bytes=VMEM_LIMIT_BYTES),
        name="mix_out",
    )(x, ya, hm, ha, g12, wm, wa, wo)


def _pad_heads(w, heads, dh):
    d = w.shape[0]
    return jnp.pad(w.reshape(d, heads, dh), ((0, 0), (0, 0), (0, LANES - dh))).reshape(d, heads * LANES)


def _mixer_weights(w_in):
    d = w_in.shape[0]
    wb = w_in.astype(BF16)
    z = lambda n: jnp.zeros((d, n), BF16)
    col = lambda o, n: wb[:, o:o + n]
    return {
        "conv": col(O_CB, 3 * D_CONV),
        "mqk": col(O_MQ, 2 * H_M * DQK_M),
        "mv": col(O_MV, D_MV),
        "mo": col(O_MO, D_MV),
        "small": jnp.concatenate([col(O_MI, 2 * H_M), col(O_IW, H_IDX), z(LANES - 2 * H_M - H_IDX)], axis=1),
        "aq": _pad_heads(col(O_AQ, H_A * DH_A) * jnp.asarray(DH_A ** -0.5, BF16), H_A, DH_A),
        "kvi": jnp.concatenate([col(O_AK, DH_A), z(LANES - DH_A), col(O_AV, DH_A), z(LANES - DH_A),
                                col(O_IK, D_IDX), z(LANES - D_IDX)], axis=1),
        "iq": _pad_heads(col(O_IQ, H_IDX * D_IDX), H_IDX, D_IDX),
        "gates": col(O_GATES, 3 * d),
    }


def kernel(x, p, norm_ffn1, ffn1_w_gu, ffn1_w_down, norm_mix, w_in, conv_w, conv_w_out, mlstm_b_i, mlstm_b_f,
           mlstm_norm, mlstm_w_out, attn_w_out, w_o, norm_ffn2, ffn2_w_gu, ffn2_w_down, norm_ple, ple_w_gate,
           ple_w_proj, final_norm):
    batch, seq, d = x.shape
    depth = w_in.shape[0]
    t = batch * seq
    xs = x.reshape(t, d)
    for l in range(depth):
        xs = _ffn(xs, norm_ffn1[l], ffn1_w_gu[l].astype(BF16), ffn1_w_down[l].astype(BF16))
        pr = _mix_in(xs, norm_mix[l], _mixer_weights(w_in[l]), conv_w[l], conv_w_out[l].astype(BF16), seq)
        bias_row = jnp.concatenate([mlstm_b_i[l], mlstm_b_f[l], jnp.zeros((LANES - 2 * H_M,), F32)]).reshape(1, LANES)
        hm = _mlstm(pr["mqk"], pr["mv"], pr["smo"], pr["small"], bias_row, mlstm_norm[l], batch, seq)
        ha = _dsa(pr["aq"], pr["iq"], pr["small"], pr["kvi"], batch, seq)
        xs = _mix_out(xs, pr["ya"], hm, ha, pr["g12"], mlstm_w_out[l].astype(BF16), attn_w_out[l].astype(BF16),
                      w_o[l].astype(BF16))
        xs = _ffn(xs, norm_ffn2[l], ffn2_w_gu[l].astype(BF16), ffn2_w_down[l].astype(BF16),
                  ple=(p.reshape(depth, t, p.shape[-1]), l, norm_ple[l], ple_w_gate[l].astype(BF16),
                       ple_w_proj[l].astype(BF16)),
                  final_g=final_norm if l == depth - 1 else None)
    return xs.reshape(batch, seq, d)
```

```python
import functools

import jax
import jax.numpy as jnp
from jax import lax
from jax.experimental import pallas as pl
from jax.experimental.pallas import tpu as pltpu

BF16 = jnp.bfloat16
F32 = jnp.float32
I32 = jnp.int32

D_PLE = 256
D_FF = 2816
D_CONV = 512
CONV_W = 3
H_M = 4
DQK_M = 128
DV_M = 256
D_MV = H_M * DV_M
H_A = 8
DH_A = 64
H_IDX = 4
D_IDX = 64
TOPK_MAX = 256
Q_BLOCK = 128
EPS = 1e-6

_SPLITS = (D_CONV, D_CONV, D_CONV, H_M * DQK_M, H_M * DQK_M, D_MV, D_MV, H_M, H_M,
           H_A * DH_A, DH_A, DH_A, H_IDX * D_IDX, D_IDX, H_IDX)
_OFF = [0]
for _s in _SPLITS:
    _OFF.append(_OFF[-1] + _s)
(O_CB, O_CC, O_CX, O_MQ, O_MK, O_MV, O_MO, O_MI, O_MF, O_AQ, O_AK, O_AV, O_IQ, O_IK, O_IW, O_GATES) = _OFF

LANES = 128
VMEM_LIMIT_BYTES = 56 * 1024 * 1024

TM_FFN = 1024
TM_MIX = 512
FF_CHUNK = 512
MLSTM_L = 256
DSA_KC = 512
INT_MIN = -2 ** 31
MASK_BIAS = -1e30
M_FLOOR = -1e29
LOG2E = 1.4426950408889634
ACC_ROWS = 64
TRANSPOSE_MASKS = (0x0000FFFF, 0x00FF00FF, 0x0F0F0F0F, 0x33333333, 0x55555555)

def _const_spec(shape):
    nd = len(shape)
    return pl.BlockSpec(shape, lambda *_: (0,) * nd, pipeline_mode=pl.Buffered(1))


def _rms(x, g):
    return x * lax.rsqrt(jnp.mean(x * x, axis=-1, keepdims=True) + EPS) * g


def _dot(a, b):
    return jnp.dot(a, b, preferred_element_type=F32)


def _dot_nt(a, b):
    return lax.dot_general(a, b, (((1,), (1,)), ((), ())), preferred_element_type=F32)


def _ffn_kernel(*refs, chunks, d_ff, with_ple, with_final):
    if with_ple:
        x_ref, g_ref, wgu_ref, wd_ref, p_ref, gp_ref, wpg_ref, wpp_ref = refs[:8]
        rest = refs[8:]
    else:
        x_ref, g_ref, wgu_ref, wd_ref = refs[:4]
        rest = refs[4:]
    if with_final:
        gf_ref, o_ref = rest
    else:
        (o_ref,) = rest
    x = x_ref[...]
    h = _rms(x, g_ref[...]).astype(BF16)
    acc = jnp.zeros_like(x)
    for off, w in chunks:
        gg = _dot(h, wgu_ref[:, off:off + w])
        uu = _dot(h, wgu_ref[:, d_ff + off:d_ff + off + w])
        a = (gg * jax.nn.sigmoid(gg) * uu).astype(BF16)
        acc = acc + _dot(a, wd_ref[off:off + w, :])
    x = x + 0.5 * acc
    if with_ple:
        gate = jax.nn.sigmoid(_dot(_rms(x, gp_ref[...]).astype(BF16), wpg_ref[...]))
        x = x + gate * _dot(p_ref[...].astype(BF16), wpp_ref[...])
    if with_final:
        x = _rms(x, gf_ref[...])
    o_ref[...] = x


def _ffn(x, g, wgu, wd, ple=None, final_g=None):
    t, d = x.shape
    d_ff = wd.shape[0]
    tm = min(TM_FFN, t)
    chunks = []
    off = 0
    while off < d_ff:
        w = min(FF_CHUNK, d_ff - off)
        chunks.append((off, w))
        off += w
    tok = lambda i: (i, 0)
    in_specs = [pl.BlockSpec((tm, d), tok), _const_spec((1, d)), _const_spec(wgu.shape), _const_spec(wd.shape)]
    args = [x, g.reshape(1, d), wgu, wd]
    if ple is not None:
        p, layer, gp, wpg, wpp = ple
        in_specs += [pl.BlockSpec((None, tm, p.shape[2]), lambda i: (layer, i, 0)), _const_spec((1, d)),
                     _const_spec(wpg.shape), _const_spec(wpp.shape)]
        args += [p, gp.reshape(1, d), wpg, wpp]
    if final_g is not None:
        in_specs.append(_const_spec((1, d)))
        args.append(final_g.reshape(1, d))
    kern = functools.partial(_ffn_kernel, chunks=tuple(chunks), d_ff=d_ff, with_ple=ple is not None,
                             with_final=final_g is not None)
    return pl.pallas_call(
        kern,
        grid=(t // tm,),
        in_specs=in_specs,
        out_specs=pl.BlockSpec((tm, d), tok),
        out_shape=jax.ShapeDtypeStruct((t, d), F32),
        compiler_params=pltpu.CompilerParams(dimension_semantics=("arbitrary",),
                                             vmem_limit_bytes=VMEM_LIMIT_BYTES),
        name="ffn",
    )(*args)


def _mix_in_kernel(x_ref, g_ref, wconv_ref, wmqk_ref, wmv_ref, wmo_ref, wsmall_ref, waq_ref, wkvi_ref, wiq_ref,
                   wgate_ref, cw_ref, cwo_ref,
                   ya_ref, mqk_ref, mv_ref, smo_ref, small_ref, aq_ref, kvi_ref, iq_ref, g12_ref,
                   ubuf_ref, *, tiles_per_seq):
    tm, d = x_ref.shape
    h = _rms(x_ref[...], g_ref[...]).astype(BF16)

    mqk_ref[...] = _dot(h, wmqk_ref[...]).astype(BF16)
    mv_ref[...] = _dot(h, wmv_ref[...]).astype(BF16)
    smo_ref[...] = jax.nn.sigmoid(_dot(h, wmo_ref[...])).astype(BF16)
    small_ref[...] = _dot(h, wsmall_ref[...])
    aq_ref[...] = (_dot(h, waq_ref[...]) * LOG2E).astype(BF16)
    kvi_ref[...] = _dot(h, wkvi_ref[...]).astype(BF16)
    iq_ref[...] = _dot(h, wiq_ref[...]).astype(BF16)
    g12_ref[...] = jax.nn.sigmoid(_dot(h, wgate_ref[:, d:3 * d])).astype(BF16)

    @pl.when(pl.program_id(0) % tiles_per_seq == 0)
    def _():
        ubuf_ref[0:8, :] = jnp.zeros((8, D_CONV), F32)

    cb = _dot(h, wconv_ref[:, 0:D_CONV])
    u = _dot(h, wconv_ref[:, D_CONV:2 * D_CONV]) * _dot(h, wconv_ref[:, 2 * D_CONV:3 * D_CONV])
    ubuf_ref[8:tm + 8, :] = u
    y = (ubuf_ref[6:tm + 6, :] * cw_ref[0:1, :] + ubuf_ref[7:tm + 7, :] * cw_ref[1:2, :] + u * cw_ref[2:3, :])
    ubuf_ref[0:8, :] = ubuf_ref[tm:tm + 8, :]
    ya = _dot((cb * y).astype(BF16), cwo_ref[...])
    ya_ref[...] = (jax.nn.sigmoid(_dot(h, wgate_ref[:, 0:d])) * ya).astype(BF16)


def _mix_in(x, g, w, conv_w, conv_w_out, seq):
    t, d = x.shape
    tm = min(TM_MIX, seq)
    tok = lambda i: (i, 0)
    weights = [w["conv"], w["mqk"], w["mv"], w["mo"], w["small"], w["aq"], w["kvi"], w["iq"], w["gates"],
               conv_w, conv_w_out]
    outs = [("ya", d, BF16), ("mqk", 2 * H_M * DQK_M, BF16), ("mv", D_MV, BF16), ("smo", D_MV, BF16),
            ("small", LANES, F32), ("aq", H_A * DH_A, BF16), ("kvi", 2 * LANES, BF16), ("iq", H_IDX * D_IDX, BF16),
            ("g12", 2 * d, BF16)]
    res = pl.pallas_call(
        functools.partial(_mix_in_kernel, tiles_per_seq=seq // tm),
        grid=(t // tm,),
        in_specs=[pl.BlockSpec((tm, d), tok), _const_spec((1, d))] + [_const_spec(a.shape) for a in weights],
        out_specs=[pl.BlockSpec((tm, n), tok) for _, n, _ in outs],
        out_shape=[jax.ShapeDtypeStruct((t, n), dt) for _, n, dt in outs],
        scratch_shapes=[pltpu.VMEM((tm + 8, D_CONV), F32)],
        compiler_params=pltpu.CompilerParams(dimension_semantics=("arbitrary",),
                                             vmem_limit_bytes=VMEM_LIMIT_BYTES),
        name="mix_in",
    )(x, g.reshape(1, d), *weights)
    return {name: r for (name, _, _), r in zip(outs, res)}


def _log_sigmoid(x):
    return jnp.minimum(x, 0.0) - jnp.log(1.0 + jnp.exp(-jnp.abs(x)))


def _mlstm_kernel(qk_ref, v_ref, smo_ref, small_ref, bias_ref, nw_ref, tri_ref, o_ref, c_ref, n_ref, m_ref):
    L = qk_ref.shape[0]
    scale = DQK_M ** -0.5

    @pl.when(pl.program_id(1) == 0)
    def _():
        c_ref[...] = jnp.zeros(c_ref.shape, F32)
        n_ref[...] = jnp.zeros(n_ref.shape, F32)
        m_ref[...] = jnp.zeros(m_ref.shape, F32)

    sm = small_ref[...] + bias_ref[...]
    lane = lax.broadcasted_iota(I32, sm.shape, 1)
    g = jnp.where((lane >= H_M) & (lane < 2 * H_M), _log_sigmoid(sm), sm)
    tri = tri_ref[...]
    g1 = g.astype(BF16)
    r1 = g - g1.astype(F32)
    g2 = r1.astype(BF16)
    g3 = (r1 - g2.astype(F32)).astype(BF16)
    cs = _dot(tri, g1) + _dot(tri, g2) + _dot(tri, g3)
    g_t = g.T
    cs_t = cs.T
    causal = lax.broadcasted_iota(I32, (L, L), 0) <= lax.broadcasted_iota(I32, (L, L), 1)

    def over_sources(x, op):
        return op(op(x.reshape(L // ACC_ROWS, ACC_ROWS, x.shape[1]), axis=0), axis=0, keepdims=True)

    for h in range(H_M):
        i_r = g_t[h:h + 1, :]
        b_r = cs_t[H_M + h:H_M + h + 1, :]
        a_c = g[:, h:h + 1] - cs[:, H_M + h:H_M + h + 1]
        m_prev = m_ref[h:h + 1, 0:1]
        dlog = jnp.where(causal, a_c + b_r, -jnp.inf)
        inter = b_r + m_prev
        m_t = jnp.maximum(inter, over_sources(dlog, jnp.max))
        q = qk_ref[:, h * DQK_M:(h + 1) * DQK_M]
        k = qk_ref[:, (H_M + h) * DQK_M:(H_M + h + 1) * DQK_M]
        v_t = v_ref[:, h * DV_M:(h + 1) * DV_M].astype(F32).T
        sc = _dot_nt(k, q) * scale * jnp.exp(dlog - m_t)
        w_int = jnp.exp(inter - m_t)
        c_state = c_ref[h]
        n_row = n_ref[h:h + 1, :]
        num = w_int * _dot_nt(c_state.astype(BF16), q) + _dot(v_t.astype(BF16), sc.astype(BF16))
        qn = _dot_nt(jnp.broadcast_to(n_row, (8, DQK_M)).astype(BF16), q)[0:1]
        den = w_int * qn + over_sources(sc, jnp.sum)
        hv = num / jnp.maximum(jnp.abs(den), jnp.exp(-m_t))
        hn = hv * lax.rsqrt(over_sources(hv * hv, jnp.sum) * (1.0 / DV_M) + EPS) * nw_ref[h * DV_M:(h + 1) * DV_M, :]
        o_ref[:, h * DV_M:(h + 1) * DV_M] = (smo_ref[:, h * DV_M:(h + 1) * DV_M].astype(F32) * hn.T).astype(BF16)
        b_last = cs_t[H_M + h:H_M + h + 1, L - 1:L]
        g_r = b_last - b_r + i_r
        m_new = jnp.maximum(b_last + m_prev, jnp.max(g_r, axis=1, keepdims=True))
        wk = jnp.exp(g_r - m_new) * scale
        decay = jnp.exp(b_last + m_prev - m_new)
        c_ref[h] = decay * c_state + _dot((v_t * wk).astype(BF16), k)
        n_ref[h:h + 1, :] = decay * n_row + _dot(jnp.broadcast_to(wk, (8, L)).astype(BF16), k)[0:1]
        m_ref[h:h + 1, :] = jnp.broadcast_to(m_new, (1, LANES))


def _mlstm(mqk, mv, smo, small, bias_row, norm_w, batch, seq):
    L = min(MLSTM_L, seq)
    r3 = lambda a: a.reshape(batch, seq, a.shape[-1])
    blk = lambda n: pl.BlockSpec((None, L, n), lambda b, c: (b, c, 0))
    tri = jnp.tril(jnp.ones((L, L), BF16))
    out = pl.pallas_call(
        _mlstm_kernel,
        grid=(batch, seq // L),
        in_specs=[blk(2 * H_M * DQK_M), blk(D_MV), blk(D_MV), blk(LANES), _const_spec((1, LANES)),
                  _const_spec((D_MV, L)), _const_spec((L, L))],
        out_specs=blk(D_MV),
        out_shape=jax.ShapeDtypeStruct((batch, seq, D_MV), BF16),
        scratch_shapes=[pltpu.VMEM((H_M, DV_M, DQK_M), F32), pltpu.VMEM((8, LANES), F32),
                        pltpu.VMEM((8, LANES), F32)],
        compiler_params=pltpu.CompilerParams(dimension_semantics=("arbitrary", "arbitrary"),
                                             vmem_limit_bytes=VMEM_LIMIT_BYTES),
        name="mlstm",
    )(r3(mqk), r3(mv), r3(smo), r3(small), bias_row, jnp.broadcast_to(norm_w.reshape(D_MV, 1), (D_MV, L)), tri)
    return out.reshape(batch * seq, D_MV)


def _dsa_kernel(aq_ref, iq_ref, small_ref, kvi_ref, tri_ref, eye_ref, o_ref,
                keys_ref, planes_ref, und_ref, s0_ref, s1_ref, p_ref, acc_ref, m_ref, *, n_sel):
    kc = keys_ref.shape[1]
    qb = pl.program_id(1)
    nkc = (qb * Q_BLOCK) // kc + 1
    t_idx = qb * Q_BLOCK + lax.broadcasted_iota(I32, (kc, Q_BLOCK), 1)
    s_iota = lax.broadcasted_iota(I32, (kc, Q_BLOCK), 0)

    def part(x):
        return x.reshape(kc // ACC_ROWS, ACC_ROWS, Q_BLOCK)

    def rows_sum(x):
        return jnp.sum(jnp.sum(part(x), axis=0), axis=0, keepdims=True)

    def head_rows(ref, heads, keep_low_only):
        low = lax.broadcasted_iota(I32, (Q_BLOCK, LANES), 1) < DH_A
        out = []
        for pair in range(heads // 2):
            blk = ref[:, pair * LANES:(pair + 1) * LANES].astype(F32)
            for x in (blk, pltpu.roll(blk, DH_A, axis=1)):
                out.append((jnp.where(low, x, 0.0) if keep_low_only else x).astype(BF16))
        return jnp.concatenate(out, axis=0)

    iq_all = head_rows(iq_ref, H_IDX, False)
    w_scale = (H_IDX ** -0.5) * (D_IDX ** -0.5)
    sm_t = small_ref[...].T
    w_rows = [sm_t[2 * H_M + h:2 * H_M + h + 1, :] * w_scale for h in range(H_IDX)]

    hk = kc // 2
    vis_margin = (t_idx - s_iota)[0:hk]

    def score_body(c, carry):
        for sub in range(2):
            s0 = c * kc + sub * hk
            rows = slice(sub * hk, (sub + 1) * hk)
            ik_c = kvi_ref[pl.ds(pl.multiple_of(s0, hk), hk), LANES:2 * LANES]
            lg = _dot_nt(ik_c, iq_all)
            sc = w_rows[0] * jnp.maximum(lg[:, 0:Q_BLOCK], 0.0)
            for h in range(1, H_IDX):
                sc = sc + w_rows[h] * jnp.maximum(lg[:, h * Q_BLOCK:(h + 1) * Q_BLOCK], 0.0)
            bits = lax.bitcast_convert_type(sc, I32)
            bits = jnp.where(bits == INT_MIN, 0, bits)
            key = bits ^ ((bits >> 31) & 0x7FFFFFFF)
            key = jnp.where(vis_margin >= s0, key, INT_MIN)
            keys_ref[c, rows, :] = key
            ukey = key ^ INT_MIN
            words = [ukey[8 * j:8 * (j + 1)] for j in range(32)]
            step = 16
            for mask in TRANSPOSE_MASKS:
                for j in range(32):
                    if j & step == 0:
                        hi = lax.shift_right_logical(words[j + step], jnp.full((8, Q_BLOCK), step, I32))
                        t = (words[j] ^ hi) & mask
                        words[j] = words[j] ^ t
                        words[j + step] = words[j + step] ^ jnp.left_shift(t, jnp.int32(step))
                step //= 2
            for i in range(32):
                planes_ref[c, i, 8 * sub:8 * (sub + 1), :] = words[i]
        und_ref[c] = jnp.full(und_ref.shape[1:], -1, I32)
        return carry

    lax.fori_loop(0, nkc, score_body, 0)

    def sweep(i_prev, flip, i_next):
        def body(c, acc):
            und = und_ref[c]
            if i_prev is not None:
                und = und & (planes_ref[c, i_prev] ^ flip)
                und_ref[c] = und
            if i_next is not None:
                und = und & planes_ref[c, i_next]
            return acc + lax.population_count(und)
        acc = lax.fori_loop(0, nkc, body, jnp.zeros(und_ref.shape[1:], I32))
        return jnp.sum(acc.astype(F32), axis=0, keepdims=True)

    def decide(i, n_set, t_bits, n_above):
        ok = n_above + n_set >= n_sel
        return (jnp.where(ok, t_bits | jnp.left_shift(jnp.int32(1), 31 - i), t_bits),
                jnp.where(ok, n_above, n_above + n_set), jnp.where(ok, 0, -1))

    def bit_body(i, state):
        t_bits, n_above, flip = state
        return decide(i, sweep(i - 1, flip, i), t_bits, n_above)

    state = decide(0, sweep(None, None, 0), jnp.zeros((1, Q_BLOCK), I32), jnp.zeros((1, Q_BLOCK), F32))
    t_bits, n_gt, flip = lax.fori_loop(1, 32, bit_body, state)
    n_eq = sweep(31, flip, None)
    thr = t_bits ^ INT_MIN
    need = n_sel - n_gt
    excess_ties = jnp.max(n_eq - need) > 0.0

    q_all = head_rows(aq_ref, H_A, True)
    q_aug = jnp.concatenate([q_all, eye_ref[...]], axis=1)
    m_ref[...] = jnp.full(m_ref.shape, M_FLOOR, F32)
    acc_ref[...] = jnp.zeros(acc_ref.shape, F32)
    ones_row = jnp.where(lax.broadcasted_iota(I32, (LANES - DH_A, kc), 0) == 0, 1.0, 0.0).astype(BF16)

    def logits(c, s_ref, tie_carry, with_ties):
        start = pl.multiple_of(c * kc, kc)
        k = keys_ref[c]
        if with_ties:
            eq = k == thr
            eq_f = jnp.where(eq, 1.0, 0.0)
            rank = _dot(tri_ref[...], eq_f.astype(BF16)) + tie_carry
            sel = ((k > thr) | (eq & (rank < need))) & (k > INT_MIN)
            tie_carry = tie_carry + rows_sum(eq_f)
        else:
            sel = (k >= thr) & (k > INT_MIN)
        bias = jnp.where(sel, 0.0, MASK_BIAS).astype(BF16)
        k_aug = jnp.concatenate([kvi_ref[pl.ds(start, kc), 0:LANES], bias], axis=1)
        s_ref[:, 0:H_A * Q_BLOCK] = _dot_nt(k_aug, q_aug)
        return tie_carry

    def accumulate(c, s_ref):
        start = pl.multiple_of(c * kc, kc)
        kv_t = kvi_ref[pl.ds(start, kc), 0:LANES].astype(F32).T
        v_t = jnp.concatenate([kv_t[DH_A:].astype(BF16), ones_row], axis=0)
        for h in range(H_A):
            cols = slice(h * Q_BLOCK, (h + 1) * Q_BLOCK)
            sh = s_ref[:, cols]
            m_old = m_ref[h:h + 1, :]
            m_new = jnp.maximum(m_old, jnp.max(jnp.max(part(sh), axis=0), axis=0, keepdims=True))
            p_ref[:, cols] = jnp.exp2(sh - m_new).astype(BF16)
            m_ref[h:h + 1, :] = m_new
            acc_ref[:, cols] = acc_ref[:, cols] * jnp.exp2(m_old - m_new)
        acc_ref[:, 0:H_A * Q_BLOCK] = acc_ref[:, 0:H_A * Q_BLOCK] + _dot(v_t, p_ref[:, 0:H_A * Q_BLOCK])

    def attend(with_ties):
        last = nkc - 1
        tie = logits(0, s0_ref, jnp.zeros((1, Q_BLOCK), F32), with_ties)

        def pair(j, tie):
            c = 2 * j
            tie = logits(c + 1, s1_ref, tie, with_ties)
            accumulate(c, s0_ref)
            tie = logits(jnp.minimum(c + 2, last), s0_ref, tie, with_ties)
            accumulate(c + 1, s1_ref)
            return tie

        lax.fori_loop(0, nkc // 2, pair, tie)

        @pl.when(nkc % 2 == 1)
        def _():
            accumulate(last, s0_ref)

    @pl.when(excess_ties)
    def _():
        attend(True)

    @pl.when(jnp.logical_not(excess_ties))
    def _():
        attend(False)

    lane = lax.broadcasted_iota(I32, (Q_BLOCK, LANES), 1)
    pairs = []
    for hp in range(H_A // 2):
        o = []
        for h in (2 * hp, 2 * hp + 1):
            cols = slice(h * Q_BLOCK, (h + 1) * Q_BLOCK)
            o.append((acc_ref[:, cols] / acc_ref[DH_A:DH_A + 1, cols]).T)
        pairs.append(jnp.where(lane < DH_A, o[0], pltpu.roll(o[1], DH_A, axis=1)))
    o_ref[...] = jnp.concatenate(pairs, axis=1).astype(BF16)


def _dsa(aq, iq, small, kvi, batch, seq):
    d = H_A * DH_A
    kc = min(DSA_KC, seq)
    n_sel = min(TOPK_MAX, seq // 4)
    r3 = lambda a: a.reshape(batch, seq, a.shape[-1])
    qblk = lambda n: pl.BlockSpec((None, Q_BLOCK, n), lambda b, q: (b, q, 0))
    tri = jnp.tril(jnp.ones((kc, kc), BF16), k=-1)
    eye = jnp.tile(jnp.eye(Q_BLOCK, dtype=BF16), (H_A, 1))
    out = pl.pallas_call(
        functools.partial(_dsa_kernel, n_sel=n_sel),
        grid=(batch, seq // Q_BLOCK),
        in_specs=[qblk(H_A * DH_A), qblk(H_IDX * D_IDX), qblk(LANES),
                  pl.BlockSpec((None, seq, 2 * LANES), lambda b, q: (b, 0, 0)),
                  _const_spec((kc, kc)), _const_spec(eye.shape)],
        out_specs=qblk(d),
        out_shape=jax.ShapeDtypeStruct((batch, seq, d), BF16),
        scratch_shapes=[pltpu.VMEM((seq // kc, kc, Q_BLOCK), I32),
                        pltpu.VMEM((seq // kc, 32, kc // 32, Q_BLOCK), I32),
                        pltpu.VMEM((seq // kc, kc // 32, Q_BLOCK), I32),
                        pltpu.VMEM((kc, H_A * Q_BLOCK + LANES), F32),
                        pltpu.VMEM((kc, H_A * Q_BLOCK + LANES), F32),
                        pltpu.VMEM((kc, H_A * Q_BLOCK + LANES), BF16),
                        pltpu.VMEM((LANES, H_A * Q_BLOCK + LANES), F32),
                        pltpu.VMEM((H_A, Q_BLOCK), F32)],
        compiler_params=pltpu.CompilerParams(dimension_semantics=("arbitrary", "arbitrary"),
                                             vmem_limit_bytes=VMEM_LIMIT_BYTES),
        name="dsa",
    )(r3(aq), r3(iq), r3(small), r3(kvi), tri, eye)
    return out.reshape(batch * seq, d)


def _mix_out_kernel(x_ref, ya_ref, hm_ref, ha_ref, g12_ref, wm_ref, wa_ref, wo_ref, o_ref):
    d = x_ref.shape[1]
    ym = _dot(hm_ref[...], wm_ref[...])
    yc = _dot(ha_ref[...], wa_ref[...])
    merged = ya_ref[...].astype(F32) + g12_ref[:, 0:d].astype(F32) * ym + g12_ref[:, d:2 * d].astype(F32) * yc
    o_ref[...] = x_ref[...] + _dot(merged.astype(BF16), wo_ref[...])


def _mix_out(x, ya, hm, ha, g12, wm, wa, wo):
    t, d = x.shape
    tm = min(TM_FFN, t)
    tok = lambda i: (i, 0)
    return pl.pallas_call(
        _mix_out_kernel,
        grid=(t // tm,),
        in_specs=[pl.BlockSpec((tm, d), tok), pl.BlockSpec((tm, d), tok), pl.BlockSpec((tm, D_MV), tok),
                  pl.BlockSpec((tm, ha.shape[1]), tok), pl.BlockSpec((tm, 2 * d), tok), _const_spec(wm.shape),
                  _const_spec(wa.shape), _const_spec(wo.shape)],
        out_specs=pl.BlockSpec((tm, d), tok),
        out_shape=jax.ShapeDtypeStruct((t, d), F32),
        compiler_params=pltpu.CompilerParams(dimension_semantics=("arbitrary",),
                                             vmem_limit_bytes=VMEM_LIMIT_BYTES),
        name="mix_out",
    )(x, ya, hm, ha, g12, wm, wa, wo)


def _mixer_weights(w_in):
    d = w_in.shape[0]
    wb = w_in.astype(BF16)
    z = lambda n: jnp.zeros((d, n), BF16)
    col = lambda o, n: wb[:, o:o + n]
    return {
        "conv": col(O_CB, 3 * D_CONV),
        "mqk": col(O_MQ, 2 * H_M * DQK_M),
        "mv": col(O_MV, D_MV),
        "mo": col(O_MO, D_MV),
        "small": jnp.concatenate([col(O_MI, 2 * H_M), col(O_IW, H_IDX), z(LANES - 2 * H_M - H_IDX)], axis=1),
        "aq": col(O_AQ, H_A * DH_A) * jnp.asarray(DH_A ** -0.5, BF16),
        "kvi": jnp.concatenate([col(O_AK, DH_A), col(O_AV, DH_A), col(O_IK, D_IDX), z(LANES - D_IDX)], axis=1),
        "iq": col(O_IQ, H_IDX * D_IDX),
        "gates": col(O_GATES, 3 * d),
    }


def kernel(x, p, norm_ffn1, ffn1_w_gu, ffn1_w_down, norm_mix, w_in, conv_w, conv_w_out, mlstm_b_i, mlstm_b_f,
           mlstm_norm, mlstm_w_out, attn_w_out, w_o, norm_ffn2, ffn2_w_gu, ffn2_w_down, norm_ple, ple_w_gate,
           ple_w_proj, final_norm):
    batch, seq, d = x.shape
    depth = w_in.shape[0]
    t = batch * seq
    xs = x.reshape(t, d)
    for l in range(depth):
        xs = _ffn(xs, norm_ffn1[l], ffn1_w_gu[l].astype(BF16), ffn1_w_down[l].astype(BF16))
        pr = _mix_in(xs, norm_mix[l], _mixer_weights(w_in[l]), conv_w[l], conv_w_out[l].astype(BF16), seq)
        bias_row = jnp.concatenate([mlstm_b_i[l], mlstm_b_f[l], jnp.zeros((LANES - 2 * H_M,), F32)]).reshape(1, LANES)
        hm = _mlstm(pr["mqk"], pr["mv"], pr["smo"], pr["small"], bias_row, mlstm_norm[l], batch, seq)
        ha = _dsa(pr["aq"], pr["iq"], pr["small"], pr["kvi"], batch, seq)
        xs = _mix_out(xs, pr["ya"], hm, ha, pr["g12"], mlstm_w_out[l].astype(BF16), attn_w_out[l].astype(BF16),
                      w_o[l].astype(BF16))
        xs = _ffn(xs, norm_ffn2[l], ffn2_w_gu[l].astype(BF16), ffn2_w_down[l].astype(BF16),
                  ple=(p.reshape(depth, t, p.shape[-1]), l, norm_ple[l], ple_w_gate[l].astype(BF16),
                       ple_w_proj[l].astype(BF16)),
                  final_g=final_norm if l == depth - 1 else None)
    return xs.reshape(batch, seq, d)
```

```python
import functools

import jax
import jax.numpy as jnp
from jax import lax
from jax.experimental import pallas as pl
from jax.experimental.pallas import tpu as pltpu

BF16 = jnp.bfloat16
F32 = jnp.float32
I32 = jnp.int32

D_PLE = 256
D_FF = 2816
D_CONV = 512
CONV_W = 3
H_M = 4
DQK_M = 128
DV_M = 256
D_MV = H_M * DV_M
H_A = 8
DH_A = 64
H_IDX = 4
D_IDX = 64
TOPK_MAX = 256
Q_BLOCK = 128
EPS = 1e-6

_SPLITS = (D_CONV, D_CONV, D_CONV, H_M * DQK_M, H_M * DQK_M, D_MV, D_MV, H_M, H_M,
           H_A * DH_A, DH_A, DH_A, H_IDX * D_IDX, D_IDX, H_IDX)
_OFF = [0]
for _s in _SPLITS:
    _OFF.append(_OFF[-1] + _s)
(O_CB, O_CC, O_CX, O_MQ, O_MK, O_MV, O_MO, O_MI, O_MF, O_AQ, O_AK, O_AV, O_IQ, O_IK, O_IW, O_GATES) = _OFF

LANES = 128
VMEM_LIMIT_BYTES = 56 * 1024 * 1024

TM_FFN = 1024
TM_MIX = 512
FF_CHUNK = 256
MLSTM_L = 256
DSA_KC = 512
INT_MIN = -2 ** 31
MASK_BIAS = -1e30
M_FLOOR = -1e29
LOG2E = 1.4426950408889634
ACC_ROWS = 64
KEY_GROUP = 256
TRANSPOSE_MASKS = (0x0000FFFF, 0x00FF00FF, 0x0F0F0F0F, 0x33333333, 0x55555555)


def _const_spec(shape):
    nd = len(shape)
    return pl.BlockSpec(shape, lambda *_: (0,) * nd, pipeline_mode=pl.Buffered(1))


def _rms(x, g):
    return x * lax.rsqrt(jnp.mean(x * x, axis=-1, keepdims=True) + EPS) * g


def _dot(a, b):
    return jnp.dot(a, b, preferred_element_type=F32)


def _dot_nt(a, b):
    return lax.dot_general(a, b, (((1,), (1,)), ((), ())), preferred_element_type=F32)


def _ffn_kernel(*refs, chunks, d_ff, with_ple, with_final):
    if with_ple:
        x_ref, g_ref, wgu_ref, wd_ref, p_ref, gp_ref, wpg_ref, wpp_ref = refs[:8]
        rest = refs[8:]
    else:
        x_ref, g_ref, wgu_ref, wd_ref = refs[:4]
        rest = refs[4:]
    if with_final:
        gf_ref, o_ref = rest
    else:
        (o_ref,) = rest
    x = x_ref[...]
    h = _rms(x, g_ref[...]).astype(BF16)
    acc = jnp.zeros_like(x)
    for off, w in chunks:
        gg = _dot(h, wgu_ref[:, off:off + w])
        uu = _dot(h, wgu_ref[:, d_ff + off:d_ff + off + w])
        a = (gg * jax.nn.sigmoid(gg) * uu).astype(BF16)
        acc = acc + _dot(a, wd_ref[off:off + w, :])
    x = x + 0.5 * acc
    if with_ple:
        gate = jax.nn.sigmoid(_dot(_rms(x, gp_ref[...]).astype(BF16), wpg_ref[...]))
        x = x + gate * _dot(p_ref[...].astype(BF16), wpp_ref[...])
    if with_final:
        x = _rms(x, gf_ref[...])
    o_ref[...] = x


def _ffn(x, g, wgu, wd, ple=None, final_g=None):
    t, d = x.shape
    d_ff = wd.shape[0]
    tm = min(TM_FFN, t)
    chunks = []
    off = 0
    while off < d_ff:
        w = min(FF_CHUNK, d_ff - off)
        chunks.append((off, w))
        off += w
    tok = lambda i: (i, 0)
    in_specs = [pl.BlockSpec((tm, d), tok), _const_spec((1, d)), _const_spec(wgu.shape), _const_spec(wd.shape)]
    args = [x, g.reshape(1, d), wgu, wd]
    if ple is not None:
        p, layer, gp, wpg, wpp = ple
        in_specs += [pl.BlockSpec((None, tm, p.shape[2]), lambda i: (layer, i, 0)), _const_spec((1, d)),
                     _const_spec(wpg.shape), _const_spec(wpp.shape)]
        args += [p, gp.reshape(1, d), wpg, wpp]
    if final_g is not None:
        in_specs.append(_const_spec((1, d)))
        args.append(final_g.reshape(1, d))
    kern = functools.partial(_ffn_kernel, chunks=tuple(chunks), d_ff=d_ff, with_ple=ple is not None,
                             with_final=final_g is not None)
    return pl.pallas_call(
        kern,
        grid=(t // tm,),
        in_specs=in_specs,
        out_specs=pl.BlockSpec((tm, d), tok),
        out_shape=jax.ShapeDtypeStruct((t, d), F32),
        compiler_params=pltpu.CompilerParams(dimension_semantics=("arbitrary",),
                                             vmem_limit_bytes=VMEM_LIMIT_BYTES),
        name="ffn",
    )(*args)


def _mix_in_kernel(x_ref, g_ref, wconv_ref, wmqk_ref, wmv_ref, wmo_ref, wsmall_ref, waq_ref, wkvi_ref, wiq_ref,
                   wgate_ref, cw_ref, cwo_ref,
                   ya_ref, mqk_ref, mv_ref, smo_ref, small_ref, aq_ref, kvi_ref, iq_ref, g12_ref,
                   ubuf_ref, *, tiles_per_seq):
    tm, d = x_ref.shape
    h = _rms(x_ref[...], g_ref[...]).astype(BF16)

    mqk_ref[...] = _dot(h, wmqk_ref[...]).astype(BF16)
    mv_ref[...] = _dot(h, wmv_ref[...]).astype(BF16)
    smo_ref[...] = jax.nn.sigmoid(_dot(h, wmo_ref[...])).astype(BF16)
    small_ref[...] = _dot(h, wsmall_ref[...])
    aq_ref[...] = (_dot(h, waq_ref[...]) * LOG2E).astype(BF16)
    kvi_ref[...] = _dot(h, wkvi_ref[...]).astype(BF16)
    iq_ref[...] = _dot(h, wiq_ref[...]).astype(BF16)
    g12_ref[...] = jax.nn.sigmoid(_dot(h, wgate_ref[:, d:3 * d])).astype(BF16)

    @pl.when(pl.program_id(0) % tiles_per_seq == 0)
    def _():
        ubuf_ref[0:8, :] = jnp.zeros((8, D_CONV), F32)

    bcx = _dot(h, wconv_ref[...])
    cb = bcx[:, 0:D_CONV]
    u = bcx[:, D_CONV:2 * D_CONV] * bcx[:, 2 * D_CONV:3 * D_CONV]
    ubuf_ref[8:tm + 8, :] = u
    y = (ubuf_ref[6:tm + 6, :] * cw_ref[0:1, :] + ubuf_ref[7:tm + 7, :] * cw_ref[1:2, :] + u * cw_ref[2:3, :])
    ubuf_ref[0:8, :] = ubuf_ref[tm:tm + 8, :]
    ya = _dot((cb * y).astype(BF16), cwo_ref[...])
    ya_ref[...] = (jax.nn.sigmoid(_dot(h, wgate_ref[:, 0:d])) * ya).astype(BF16)


def _mix_in(x, g, w, conv_w, conv_w_out, seq):
    t, d = x.shape
    tm = min(TM_MIX, seq)
    tok = lambda i: (i, 0)
    weights = [w["conv"], w["mqk"], w["mv"], w["mo"], w["small"], w["aq"], w["kvi"], w["iq"], w["gates"],
               conv_w, conv_w_out]
    outs = [("ya", d, BF16), ("mqk", 2 * H_M * DQK_M, BF16), ("mv", D_MV, BF16), ("smo", D_MV, BF16),
            ("small", LANES, F32), ("aq", H_A * DH_A, BF16), ("kvi", 2 * LANES, BF16), ("iq", H_IDX * D_IDX, BF16),
            ("g12", 2 * d, BF16)]
    res = pl.pallas_call(
        functools.partial(_mix_in_kernel, tiles_per_seq=seq // tm),
        grid=(t // tm,),
        in_specs=[pl.BlockSpec((tm, d), tok), _const_spec((1, d))] + [_const_spec(a.shape) for a in weights],
        out_specs=[pl.BlockSpec((tm, n), tok) for _, n, _ in outs],
        out_shape=[jax.ShapeDtypeStruct((t, n), dt) for _, n, dt in outs],
        scratch_shapes=[pltpu.VMEM((tm + 8, D_CONV), F32)],
        compiler_params=pltpu.CompilerParams(dimension_semantics=("arbitrary",),
                                             vmem_limit_bytes=VMEM_LIMIT_BYTES),
        name="mix_in",
    )(x, g.reshape(1, d), *weights)
    return {name: r for (name, _, _), r in zip(outs, res)}


def _log_sigmoid(x):
    return jnp.minimum(x, 0.0) - jnp.log(1.0 + jnp.exp(-jnp.abs(x)))


def _mlstm_kernel(qk_ref, v_ref, smo_ref, small_ref, bias_ref, nw_ref, tri_ref, o_ref, c_ref, n_ref, m_ref):
    L = qk_ref.shape[0]
    scale = DQK_M ** -0.5

    @pl.when(pl.program_id(1) == 0)
    def _():
        c_ref[...] = jnp.zeros(c_ref.shape, F32)
        n_ref[...] = jnp.zeros(n_ref.shape, F32)
        m_ref[...] = jnp.zeros(m_ref.shape, F32)

    sm = small_ref[...] + bias_ref[...]
    lane = lax.broadcasted_iota(I32, sm.shape, 1)
    g = jnp.where((lane >= H_M) & (lane < 2 * H_M), _log_sigmoid(sm), sm)
    tri = tri_ref[...]
    g1 = g.astype(BF16)
    r1 = g - g1.astype(F32)
    g2 = r1.astype(BF16)
    g3 = (r1 - g2.astype(F32)).astype(BF16)
    cs = _dot(tri, g1) + _dot(tri, g2) + _dot(tri, g3)
    g_t = g.T
    cs_t = cs.T
    causal = lax.broadcasted_iota(I32, (L, L), 0) <= lax.broadcasted_iota(I32, (L, L), 1)

    def over_sources(x, op):
        return op(op(x.reshape(x.shape[0] // ACC_ROWS, ACC_ROWS, x.shape[1]), axis=0), axis=0, keepdims=True)

    for h in range(H_M):
        i_r = g_t[h:h + 1, :]
        b_r = cs_t[H_M + h:H_M + h + 1, :]
        a_c = g[:, h:h + 1] - cs[:, H_M + h:H_M + h + 1]
        m_prev = m_ref[h:h + 1, 0:1]
        dlog = jnp.where(causal, a_c + b_r, -jnp.inf)
        inter = b_r + m_prev
        m_t = jnp.maximum(inter, over_sources(dlog, jnp.max))
        q = qk_ref[:, h * DQK_M:(h + 1) * DQK_M]
        k = qk_ref[:, (H_M + h) * DQK_M:(H_M + h + 1) * DQK_M]
        v_t = v_ref[:, h * DV_M:(h + 1) * DV_M].astype(F32).T
        sc = _dot_nt(k, q) * scale * jnp.exp(dlog - m_t)
        w_int = jnp.exp(inter - m_t)
        c_state = c_ref[h]
        n_row = n_ref[h:h + 1, :]
        num = w_int * _dot_nt(c_state.astype(BF16), q) + _dot(v_t.astype(BF16), sc.astype(BF16))
        qn = _dot_nt(jnp.broadcast_to(n_row, (8, DQK_M)).astype(BF16), q)[0:1]
        den = w_int * qn + over_sources(sc, jnp.sum)
        hv = num / jnp.maximum(jnp.abs(den), jnp.exp(-m_t))
        hn = hv * lax.rsqrt(over_sources(hv * hv, jnp.sum) * (1.0 / DV_M) + EPS) * nw_ref[h * DV_M:(h + 1) * DV_M, :]
        o_ref[:, h * DV_M:(h + 1) * DV_M] = (smo_ref[:, h * DV_M:(h + 1) * DV_M].astype(F32) * hn.T).astype(BF16)
        b_last = cs_t[H_M + h:H_M + h + 1, L - 1:L]
        g_r = b_last - b_r + i_r
        m_new = jnp.maximum(b_last + m_prev, jnp.max(g_r, axis=1, keepdims=True))
        wk = jnp.exp(g_r - m_new) * scale
        decay = jnp.exp(b_last + m_prev - m_new)
        c_ref[h] = decay * c_state + _dot((v_t * wk).astype(BF16), k)
        n_ref[h:h + 1, :] = decay * n_row + _dot(jnp.broadcast_to(wk, (8, L)).astype(BF16), k)[0:1]
        m_ref[h:h + 1, :] = jnp.broadcast_to(m_new, (1, LANES))


def _mlstm(mqk, mv, smo, small, bias_row, norm_w, batch, seq):
    L = min(MLSTM_L, seq)
    r3 = lambda a: a.reshape(batch, seq, a.shape[-1])
    blk = lambda n: pl.BlockSpec((None, L, n), lambda b, c: (b, c, 0))
    tri = jnp.tril(jnp.ones((L, L), BF16))
    out = pl.pallas_call(
        _mlstm_kernel,
        grid=(batch, seq // L),
        in_specs=[blk(2 * H_M * DQK_M), blk(D_MV), blk(D_MV), blk(LANES), _const_spec((1, LANES)),
                  _const_spec((D_MV, L)), _const_spec((L, L))],
        out_specs=blk(D_MV),
        out_shape=jax.ShapeDtypeStruct((batch, seq, D_MV), BF16),
        scratch_shapes=[pltpu.VMEM((H_M, DV_M, DQK_M), F32), pltpu.VMEM((8, LANES), F32),
                        pltpu.VMEM((8, LANES), F32)],
        compiler_params=pltpu.CompilerParams(dimension_semantics=("arbitrary", "arbitrary"),
                                             vmem_limit_bytes=VMEM_LIMIT_BYTES),
        name="mlstm",
    )(r3(mqk), r3(mv), r3(smo), r3(small), bias_row, jnp.broadcast_to(norm_w.reshape(D_MV, 1), (D_MV, L)), tri)
    return out.reshape(batch * seq, D_MV)


def _dsa_kernel(aq_ref, iq_ref, small_ref, kvi_ref, tri_ref, eye_ref, o_ref,
                keys_ref, planes_ref, und_ref, s0_ref, s1_ref, p_ref, acc_ref, m_ref, *, n_sel):
    kc = keys_ref.shape[1]
    qb = pl.program_id(1)
    nkc = (qb * Q_BLOCK) // kc + 1
    t_idx = qb * Q_BLOCK + lax.broadcasted_iota(I32, (kc, Q_BLOCK), 1)
    s_iota = lax.broadcasted_iota(I32, (kc, Q_BLOCK), 0)

    def part(x):
        return x.reshape(kc // ACC_ROWS, ACC_ROWS, Q_BLOCK)

    def rows_sum(x):
        return jnp.sum(jnp.sum(part(x), axis=0), axis=0, keepdims=True)

    def head_rows(ref, heads, keep_low_only):
        low = lax.broadcasted_iota(I32, (Q_BLOCK, LANES), 1) < DH_A
        out = []
        for pair in range(heads // 2):
            blk = ref[:, pair * LANES:(pair + 1) * LANES].astype(F32)
            for x in (blk, pltpu.roll(blk, DH_A, axis=1)):
                out.append((jnp.where(low, x, 0.0) if keep_low_only else x).astype(BF16))
        return jnp.concatenate(out, axis=0)

    iq_all = head_rows(iq_ref, H_IDX, False)
    w_scale = (H_IDX ** -0.5) * (D_IDX ** -0.5)
    sm_t = small_ref[...].T
    w_rows = [sm_t[2 * H_M + h:2 * H_M + h + 1, :] * w_scale for h in range(H_IDX)]

    hk = KEY_GROUP
    vis_margin = (t_idx - s_iota)[0:hk]

    def make_keys(c, sub):
        s0 = c * kc + sub * hk
        ik = kvi_ref[pl.ds(pl.multiple_of(s0, hk), hk), LANES:2 * LANES]
        lg = _dot_nt(ik, iq_all)
        sc = w_rows[0] * jnp.maximum(lg[:, 0:Q_BLOCK], 0.0)
        for h in range(1, H_IDX):
            sc = sc + w_rows[h] * jnp.maximum(lg[:, h * Q_BLOCK:(h + 1) * Q_BLOCK], 0.0)
        bits = lax.bitcast_convert_type(sc, I32)
        bits = jnp.where(bits == INT_MIN, 0, bits)
        key = bits ^ ((bits >> 31) & 0x7FFFFFFF)
        key = jnp.where(vis_margin >= s0, key, INT_MIN)
        keys_ref[c, sub * hk:(sub + 1) * hk, :] = key
        ukey = key ^ INT_MIN
        words = [ukey[8 * j:8 * (j + 1)] for j in range(32)]
        step = 16
        for mask in TRANSPOSE_MASKS:
            for j in range(32):
                if j & step == 0:
                    hi = lax.shift_right_logical(words[j + step], jnp.full((8, Q_BLOCK), step, I32))
                    t = (words[j] ^ hi) & mask
                    words[j] = words[j] ^ t
                    words[j + step] = words[j + step] ^ jnp.left_shift(t, jnp.int32(step))
            step //= 2
        for i in range(32):
            planes_ref[c, i, 8 * sub:8 * (sub + 1), :] = words[i]

    def score_body(c, carry):
        for sub in range(kc // hk):
            make_keys(c, sub)
        und_ref[c] = jnp.full(und_ref.shape[1:], -1, I32)
        return carry

    lax.fori_loop(0, nkc, score_body, 0)

    def sweep(i_prev, flip, i_next):
        def body(c, acc):
            und = und_ref[c]
            if i_prev is not None:
                und = und & (planes_ref[c, i_prev] ^ flip)
                und_ref[c] = und
            if i_next is not None:
                und = und & planes_ref[c, i_next]
            return acc + lax.population_count(und)
        acc = lax.fori_loop(0, nkc, body, jnp.zeros(und_ref.shape[1:], I32))
        return jnp.sum(acc.astype(F32), axis=0, keepdims=True)

    def decide(i, n_set, t_bits, n_above):
        ok = n_above + n_set >= n_sel
        return (jnp.where(ok, t_bits | jnp.left_shift(jnp.int32(1), 31 - i), t_bits),
                jnp.where(ok, n_above, n_above + n_set), jnp.where(ok, 0, -1))

    def bit_body(i, state):
        t_bits, n_above, flip = state
        return decide(i, sweep(i - 1, flip, i), t_bits, n_above)

    state = decide(0, sweep(None, None, 0), jnp.zeros((1, Q_BLOCK), I32), jnp.zeros((1, Q_BLOCK), F32))
    t_bits, n_gt, flip = lax.fori_loop(1, 32, bit_body, state)
    n_eq = sweep(31, flip, None)
    thr = t_bits ^ INT_MIN
    need = n_sel - n_gt
    excess_ties = jnp.max(n_eq - need) > 0.0

    q_all = head_rows(aq_ref, H_A, True)
    q_aug = jnp.concatenate([q_all, eye_ref[...]], axis=1)
    m_ref[...] = jnp.full(m_ref.shape, M_FLOOR, F32)
    acc_ref[...] = jnp.zeros(acc_ref.shape, F32)
    ones_row = jnp.where(lax.broadcasted_iota(I32, (LANES - DH_A, kc), 0) == 0, 1.0, 0.0).astype(BF16)

    def logits(c, s_ref, tie_carry, with_ties):
        start = pl.multiple_of(c * kc, kc)
        k = keys_ref[c]
        if with_ties:
            eq = k == thr
            eq_f = jnp.where(eq, 1.0, 0.0)
            rank = _dot(tri_ref[...], eq_f.astype(BF16)) + tie_carry
            sel = ((k > thr) | (eq & (rank < need))) & (k > INT_MIN)
            tie_carry = tie_carry + rows_sum(eq_f)
        else:
            sel = (k >= thr) & (k > INT_MIN)
        bias = jnp.where(sel, 0.0, MASK_BIAS).astype(BF16)
        k_aug = jnp.concatenate([kvi_ref[pl.ds(start, kc), 0:LANES], bias], axis=1)
        s_ref[:, 0:H_A * Q_BLOCK] = _dot_nt(k_aug, q_aug)
        return tie_carry

    def accumulate(c, s_ref):
        start = pl.multiple_of(c * kc, kc)
        kv_t = kvi_ref[pl.ds(start, kc), 0:LANES].astype(F32).T
        v_t = jnp.concatenate([kv_t[DH_A:].astype(BF16), ones_row], axis=0)
        rescale = []
        for h in range(H_A):
            cols = slice(h * Q_BLOCK, (h + 1) * Q_BLOCK)
            sh = s_ref[:, cols]
            m_old = m_ref[h:h + 1, :]
            m_new = jnp.maximum(m_old, jnp.max(jnp.max(part(sh), axis=0), axis=0, keepdims=True))
            p_ref[:, cols] = jnp.exp2(sh - m_new).astype(BF16)
            m_ref[h:h + 1, :] = m_new
            rescale.append(jnp.exp2(m_old - m_new))
        acc_ref[:, 0:H_A * Q_BLOCK] = (acc_ref[:, 0:H_A * Q_BLOCK] * jnp.concatenate(rescale, axis=1)
                                       + _dot(v_t, p_ref[:, 0:H_A * Q_BLOCK]))

    def attend(with_ties):
        last = nkc - 1
        tie = logits(0, s0_ref, jnp.zeros((1, Q_BLOCK), F32), with_ties)

        def pair(j, tie):
            c = 2 * j
            tie = logits(c + 1, s1_ref, tie, with_ties)
            accumulate(c, s0_ref)
            tie = logits(jnp.minimum(c + 2, last), s0_ref, tie, with_ties)
            accumulate(c + 1, s1_ref)
            return tie

        lax.fori_loop(0, nkc // 2, pair, tie)

        @pl.when(nkc % 2 == 1)
        def _():
            accumulate(last, s0_ref)

    @pl.when(excess_ties)
    def _():
        attend(True)

    @pl.when(jnp.logical_not(excess_ties))
    def _():
        attend(False)

    lane = lax.broadcasted_iota(I32, (Q_BLOCK, LANES), 1)
    pairs = []
    for hp in range(H_A // 2):
        o = []
        for h in (2 * hp, 2 * hp + 1):
            cols = slice(h * Q_BLOCK, (h + 1) * Q_BLOCK)
            o.append((acc_ref[:, cols] / acc_ref[DH_A:DH_A + 1, cols]).T)
        pairs.append(jnp.where(lane < DH_A, o[0], pltpu.roll(o[1], DH_A, axis=1)))
    o_ref[...] = jnp.concatenate(pairs, axis=1).astype(BF16)


def _dsa(aq, iq, small, kvi, batch, seq):
    d = H_A * DH_A
    kc = min(DSA_KC, seq)
    assert kc % KEY_GROUP == 0 and seq % kc == 0
    n_sel = min(TOPK_MAX, seq // 4)
    r3 = lambda a: a.reshape(batch, seq, a.shape[-1])
    qblk = lambda n: pl.BlockSpec((None, Q_BLOCK, n), lambda b, q: (b, q, 0))
    tri = jnp.tril(jnp.ones((kc, kc), BF16), k=-1)
    eye = jnp.tile(jnp.eye(Q_BLOCK, dtype=BF16), (H_A, 1))
    out = pl.pallas_call(
        functools.partial(_dsa_kernel, n_sel=n_sel),
        grid=(batch, seq // Q_BLOCK),
        in_specs=[qblk(H_A * DH_A), qblk(H_IDX * D_IDX), qblk(LANES),
                  pl.BlockSpec((None, seq, 2 * LANES), lambda b, q: (b, 0, 0)),
                  _const_spec((kc, kc)), _const_spec(eye.shape)],
        out_specs=qblk(d),
        out_shape=jax.ShapeDtypeStruct((batch, seq, d), BF16),
        scratch_shapes=[pltpu.VMEM((seq // kc, kc, Q_BLOCK), I32),
                        pltpu.VMEM((seq // kc, 32, kc // 32, Q_BLOCK), I32),
                        pltpu.VMEM((seq // kc, kc // 32, Q_BLOCK), I32),
                        pltpu.VMEM((kc, H_A * Q_BLOCK + LANES), F32),
                        pltpu.VMEM((kc, H_A * Q_BLOCK + LANES), F32),
                        pltpu.VMEM((kc, H_A * Q_BLOCK + LANES), BF16),
                        pltpu.VMEM((LANES, H_A * Q_BLOCK + LANES), F32),
                        pltpu.VMEM((H_A, Q_BLOCK), F32)],
        compiler_params=pltpu.CompilerParams(dimension_semantics=("arbitrary", "arbitrary"),
                                             vmem_limit_bytes=VMEM_LIMIT_BYTES),
        name="dsa",
    )(r3(aq), r3(iq), r3(small), r3(kvi), tri, eye)
    return out.reshape(batch * seq, d)


def _mix_out_kernel(x_ref, ya_ref, hm_ref, ha_ref, g12_ref, wm_ref, wa_ref, wo_ref, o_ref):
    d = x_ref.shape[1]
    ym = _dot(hm_ref[...], wm_ref[...])
    yc = _dot(ha_ref[...], wa_ref[...])
    merged = ya_ref[...].astype(F32) + g12_ref[:, 0:d].astype(F32) * ym + g12_ref[:, d:2 * d].astype(F32) * yc
    o_ref[...] = x_ref[...] + _dot(merged.astype(BF16), wo_ref[...])


def _mix_out(x, ya, hm, ha, g12, wm, wa, wo):
    t, d = x.shape
    tm = min(TM_FFN, t)
    tok = lambda i: (i, 0)
    return pl.pallas_call(
        _mix_out_kernel,
        grid=(t // tm,),
        in_specs=[pl.BlockSpec((tm, d), tok), pl.BlockSpec((tm, d), tok), pl.BlockSpec((tm, D_MV), tok),
                  pl.BlockSpec((tm, ha.shape[1]), tok), pl.BlockSpec((tm, 2 * d), tok), _const_spec(wm.shape),
                  _const_spec(wa.shape), _const_spec(wo.shape)],
        out_specs=pl.BlockSpec((tm, d), tok),
        out_shape=jax.ShapeDtypeStruct((t, d), F32),
        compiler_params=pltpu.CompilerParams(dimension_semantics=("arbitrary",),
                                             vmem_limit_bytes=VMEM_LIMIT_BYTES),
        name="mix_out",
    )(x, ya, hm, ha, g12, wm, wa, wo)


def _mixer_weights(w_in):
    d = w_in.shape[0]
    wb = w_in.astype(BF16)
    z = lambda n: jnp.zeros((d, n), BF16)
    col = lambda o, n: wb[:, o:o + n]
    return {
        "conv": col(O_CB, 3 * D_CONV),
        "mqk": col(O_MQ, 2 * H_M * DQK_M),
        "mv": col(O_MV, D_MV),
        "mo": col(O_MO, D_MV),
        "small": jnp.concatenate([col(O_MI, 2 * H_M), col(O_IW, H_IDX), z(LANES - 2 * H_M - H_IDX)], axis=1),
        "aq": col(O_AQ, H_A * DH_A) * jnp.asarray(DH_A ** -0.5, BF16),
        "kvi": jnp.concatenate([col(O_AK, DH_A), col(O_AV, DH_A), col(O_IK, D_IDX), z(LANES - D_IDX)], axis=1),
        "iq": col(O_IQ, H_IDX * D_IDX),
        "gates": col(O_GATES, 3 * d),
    }


def kernel(x, p, norm_ffn1, ffn1_w_gu, ffn1_w_down, norm_mix, w_in, conv_w, conv_w_out, mlstm_b_i, mlstm_b_f,
           mlstm_norm, mlstm_w_out, attn_w_out, w_o, norm_ffn2, ffn2_w_gu, ffn2_w_down, norm_ple, ple_w_gate,
           ple_w_proj, final_norm):
    batch, seq, d = x.shape
    depth = w_in.shape[0]
    t = batch * seq
    xs = x.reshape(t, d)
    for l in range(depth):
        xs = _ffn(xs, norm_ffn1[l], ffn1_w_gu[l].astype(BF16), ffn1_w_down[l].astype(BF16))
        pr = _mix_in(xs, norm_mix[l], _mixer_weights(w_in[l]), conv_w[l], conv_w_out[l].astype(BF16), seq)
        bias_row = jnp.concatenate([mlstm_b_i[l], mlstm_b_f[l], jnp.zeros((LANES - 2 * H_M,), F32)]).reshape(1, LANES)
        hm = _mlstm(pr["mqk"], pr["mv"], pr["smo"], pr["small"], bias_row, mlstm_norm[l], batch, seq)
        ha = _dsa(pr["aq"], pr["iq"], pr["small"], pr["kvi"], batch, seq)
        xs = _mix_out(xs, pr["ya"], hm, ha, pr["g12"], mlstm_w_out[l].astype(BF16), attn_w_out[l].astype(BF16),
                      w_o[l].astype(BF16))
        xs = _ffn(xs, norm_ffn2[l], ffn2_w_gu[l].astype(BF16), ffn2_w_down[l].astype(BF16),
                  ple=(p.reshape(depth, t, p.shape[-1]), l, norm_ple[l], ple_w_gate[l].astype(BF16),
                       ple_w_proj[l].astype(BF16)),
                  final_g=final_norm if l == depth - 1 else None)
    return xs.reshape(batch, seq, d)
```

```python
import functools

import jax
import jax.numpy as jnp
from jax import lax
from jax.experimental import pallas as pl
from jax.experimental.pallas import tpu as pltpu

BF16 = jnp.bfloat16
F32 = jnp.float32
I32 = jnp.int32

D_PLE = 256
D_FF = 2816
D_CONV = 512
CONV_W = 3
H_M = 4
DQK_M = 128
DV_M = 256
D_MV = H_M * DV_M
H_A = 8
DH_A = 64
H_IDX = 4
D_IDX = 64
TOPK_MAX = 256
Q_BLOCK = 128
EPS = 1e-6

_SPLITS = (D_CONV, D_CONV, D_CONV, H_M * DQK_M, H_M * DQK_M, D_MV, D_MV, H_M, H_M,
           H_A * DH_A, DH_A, DH_A, H_IDX * D_IDX, D_IDX, H_IDX)
_OFF = [0]
for _s in _SPLITS:
    _OFF.append(_OFF[-1] + _s)
(O_CB, O_CC, O_CX, O_MQ, O_MK, O_MV, O_MO, O_MI, O_MF, O_AQ, O_AK, O_AV, O_IQ, O_IK, O_IW, O_GATES) = _OFF

LANES = 128
VMEM_LIMIT_BYTES = 56 * 1024 * 1024

TM_FFN = 1024
TM_MIX = 512
FF_CHUNK = 256
MLSTM_L = 256
DSA_KC = 512
INT_MIN = -2 ** 31
MASK_BIAS = -1e30
M_FLOOR = -1e29
LOG2E = 1.4426950408889634
ACC_ROWS = 64
KEY_GROUP = 256
TRANSPOSE_MASKS = (0x0000FFFF, 0x00FF00FF, 0x0F0F0F0F, 0x33333333, 0x55555555)


def _const_spec(shape):
    nd = len(shape)
    return pl.BlockSpec(shape, lambda *_: (0,) * nd, pipeline_mode=pl.Buffered(1))


def _rms(x, g):
    return x * lax.rsqrt(jnp.mean(x * x, axis=-1, keepdims=True) + EPS) * g


def _dot(a, b):
    return jnp.dot(a, b, preferred_element_type=F32)


def _dot_nt(a, b):
    return lax.dot_general(a, b, (((1,), (1,)), ((), ())), preferred_element_type=F32)


def _ffn_kernel(*refs, chunks, d_ff, with_ple, with_final):
    if with_ple:
        x_ref, g_ref, wgu_ref, wd_ref, p_ref, gp_ref, wpg_ref, wpp_ref = refs[:8]
        rest = refs[8:]
    else:
        x_ref, g_ref, wgu_ref, wd_ref = refs[:4]
        rest = refs[4:]
    if with_final:
        gf_ref, o_ref = rest
    else:
        (o_ref,) = rest
    x = x_ref[...]
    h = _rms(x, g_ref[...]).astype(BF16)
    acc = jnp.zeros_like(x)
    for off, w in chunks:
        gg = _dot(h, wgu_ref[:, off:off + w])
        uu = _dot(h, wgu_ref[:, d_ff + off:d_ff + off + w])
        a = (gg * jax.nn.sigmoid(gg) * uu).astype(BF16)
        acc = acc + _dot(a, wd_ref[off:off + w, :])
    x = x + 0.5 * acc
    if with_ple:
        gate = jax.nn.sigmoid(_dot(_rms(x, gp_ref[...]).astype(BF16), wpg_ref[...]))
        x = x + gate * _dot(p_ref[...].astype(BF16), wpp_ref[...])
    if with_final:
        x = _rms(x, gf_ref[...])
    o_ref[...] = x


def _ffn(x, g, wgu, wd, ple=None, final_g=None):
    t, d = x.shape
    d_ff = wd.shape[0]
    tm = min(TM_FFN, t)
    chunks = []
    off = 0
    while off < d_ff:
        w = min(FF_CHUNK, d_ff - off)
        chunks.append((off, w))
        off += w
    tok = lambda i: (i, 0)
    in_specs = [pl.BlockSpec((tm, d), tok), _const_spec((1, d)), _const_spec(wgu.shape), _const_spec(wd.shape)]
    args = [x, g.reshape(1, d), wgu, wd]
    if ple is not None:
        p, layer, gp, wpg, wpp = ple
        in_specs += [pl.BlockSpec((None, tm, p.shape[2]), lambda i: (layer, i, 0)), _const_spec((1, d)),
                     _const_spec(wpg.shape), _const_spec(wpp.shape)]
        args += [p, gp.reshape(1, d), wpg, wpp]
    if final_g is not None:
        in_specs.append(_const_spec((1, d)))
        args.append(final_g.reshape(1, d))
    kern = functools.partial(_ffn_kernel, chunks=tuple(chunks), d_ff=d_ff, with_ple=ple is not None,
                             with_final=final_g is not None)
    return pl.pallas_call(
        kern,
        grid=(t // tm,),
        in_specs=in_specs,
        out_specs=pl.BlockSpec((tm, d), tok),
        out_shape=jax.ShapeDtypeStruct((t, d), F32),
        compiler_params=pltpu.CompilerParams(dimension_semantics=("arbitrary",),
                                             vmem_limit_bytes=VMEM_LIMIT_BYTES),
        name="ffn",
    )(*args)


def _mix_in_kernel(x_ref, g_ref, wconv_ref, wmqk_ref, wmv_ref, wmo_ref, wsmall_ref, waq_ref, wkvi_ref, wiq_ref,
                   wgate_ref, cw_ref, cwo_ref,
                   ya_ref, mqk_ref, mv_ref, smo_ref, small_ref, aq_ref, kvi_ref, iq_ref, g12_ref,
                   ubuf_ref, *, tiles_per_seq):
    tm, d = x_ref.shape
    h = _rms(x_ref[...], g_ref[...]).astype(BF16)

    mqk_ref[...] = _dot(h, wmqk_ref[...]).astype(BF16)
    mv_ref[...] = _dot(h, wmv_ref[...]).astype(BF16)
    smo_ref[...] = jax.nn.sigmoid(_dot(h, wmo_ref[...])).astype(BF16)
    small_ref[...] = _dot(h, wsmall_ref[...])
    aq_ref[...] = (_dot(h, waq_ref[...]) * LOG2E).astype(BF16)
    kvi_ref[...] = _dot(h, wkvi_ref[...]).astype(BF16)
    iq_ref[...] = _dot(h, wiq_ref[...]).astype(BF16)
    g12_ref[...] = jax.nn.sigmoid(_dot(h, wgate_ref[:, d:3 * d])).astype(BF16)

    @pl.when(pl.program_id(0) % tiles_per_seq == 0)
    def _():
        ubuf_ref[0:8, :] = jnp.zeros((8, D_CONV), F32)

    bcx = _dot(h, wconv_ref[...])
    cb = bcx[:, 0:D_CONV]
    u = bcx[:, D_CONV:2 * D_CONV] * bcx[:, 2 * D_CONV:3 * D_CONV]
    ubuf_ref[8:tm + 8, :] = u
    y = (ubuf_ref[6:tm + 6, :] * cw_ref[0:1, :] + ubuf_ref[7:tm + 7, :] * cw_ref[1:2, :] + u * cw_ref[2:3, :])
    ubuf_ref[0:8, :] = ubuf_ref[tm:tm + 8, :]
    ya = _dot((cb * y).astype(BF16), cwo_ref[...])
    ya_ref[...] = (jax.nn.sigmoid(_dot(h, wgate_ref[:, 0:d])) * ya).astype(BF16)


def _mix_in(x, g, w, conv_w, conv_w_out, seq):
    t, d = x.shape
    tm = min(TM_MIX, seq)
    tok = lambda i: (i, 0)
    weights = [w["conv"], w["mqk"], w["mv"], w["mo"], w["small"], w["aq"], w["kvi"], w["iq"], w["gates"],
               conv_w, conv_w_out]
    outs = [("ya", d, BF16), ("mqk", 2 * H_M * DQK_M, BF16), ("mv", D_MV, BF16), ("smo", D_MV, BF16),
            ("small", LANES, F32), ("aq", H_A * DH_A, BF16), ("kvi", 2 * LANES, BF16), ("iq", H_IDX * D_IDX, BF16),
            ("g12", 2 * d, BF16)]
    res = pl.pallas_call(
        functools.partial(_mix_in_kernel, tiles_per_seq=seq // tm),
        grid=(t // tm,),
        in_specs=[pl.BlockSpec((tm, d), tok), _const_spec((1, d))] + [_const_spec(a.shape) for a in weights],
        out_specs=[pl.BlockSpec((tm, n), tok) for _, n, _ in outs],
        out_shape=[jax.ShapeDtypeStruct((t, n), dt) for _, n, dt in outs],
        scratch_shapes=[pltpu.VMEM((tm + 8, D_CONV), F32)],
        compiler_params=pltpu.CompilerParams(dimension_semantics=("arbitrary",),
                                             vmem_limit_bytes=VMEM_LIMIT_BYTES),
        name="mix_in",
    )(x, g.reshape(1, d), *weights)
    return {name: r for (name, _, _), r in zip(outs, res)}


def _log_sigmoid(x):
    return jnp.minimum(x, 0.0) - jnp.log(1.0 + jnp.exp(-jnp.abs(x)))


def _mlstm_kernel(qk_ref, v_ref, smo_ref, small_ref, bias_ref, nw_ref, tri_ref, o_ref, c_ref, n_ref, m_ref):
    L = qk_ref.shape[0]
    scale = DQK_M ** -0.5

    @pl.when(pl.program_id(1) == 0)
    def _():
        c_ref[...] = jnp.zeros(c_ref.shape, F32)
        n_ref[...] = jnp.zeros(n_ref.shape, F32)
        m_ref[...] = jnp.zeros(m_ref.shape, F32)

    sm = small_ref[...] + bias_ref[...]
    lane = lax.broadcasted_iota(I32, sm.shape, 1)
    g = jnp.where((lane >= H_M) & (lane < 2 * H_M), _log_sigmoid(sm), sm)
    tri = tri_ref[...]
    g1 = g.astype(BF16)
    r1 = g - g1.astype(F32)
    g2 = r1.astype(BF16)
    g3 = (r1 - g2.astype(F32)).astype(BF16)
    cs = _dot(tri, g1) + _dot(tri, g2) + _dot(tri, g3)
    g_t = g.T
    cs_t = cs.T
    causal = lax.broadcasted_iota(I32, (L, L), 0) <= lax.broadcasted_iota(I32, (L, L), 1)

    def over_sources(x, op):
        return op(op(x.reshape(x.shape[0] // ACC_ROWS, ACC_ROWS, x.shape[1]), axis=0), axis=0, keepdims=True)

    for h in range(H_M):
        i_r = g_t[h:h + 1, :]
        b_r = cs_t[H_M + h:H_M + h + 1, :]
        a_c = g[:, h:h + 1] - cs[:, H_M + h:H_M + h + 1]
        m_prev = m_ref[h:h + 1, 0:1]
        dlog = jnp.where(causal, a_c + b_r, -jnp.inf)
        inter = b_r + m_prev
        m_t = jnp.maximum(inter, over_sources(dlog, jnp.max))
        q = qk_ref[:, h * DQK_M:(h + 1) * DQK_M]
        k = qk_ref[:, (H_M + h) * DQK_M:(H_M + h + 1) * DQK_M]
        v_t = v_ref[:, h * DV_M:(h + 1) * DV_M].astype(F32).T
        sc = _dot_nt(k, q) * scale * jnp.exp(dlog - m_t)
        w_int = jnp.exp(inter - m_t)
        c_state = c_ref[h]
        n_row = n_ref[h:h + 1, :]
        num = w_int * _dot_nt(c_state.astype(BF16), q) + _dot(v_t.astype(BF16), sc.astype(BF16))
        qn = _dot_nt(jnp.broadcast_to(n_row, (8, DQK_M)).astype(BF16), q)[0:1]
        den = w_int * qn + over_sources(sc, jnp.sum)
        hv = num / jnp.maximum(jnp.abs(den), jnp.exp(-m_t))
        hn = hv * lax.rsqrt(over_sources(hv * hv, jnp.sum) * (1.0 / DV_M) + EPS) * nw_ref[h * DV_M:(h + 1) * DV_M, :]
        o_ref[:, h * DV_M:(h + 1) * DV_M] = (smo_ref[:, h * DV_M:(h + 1) * DV_M].astype(F32) * hn.T).astype(BF16)
        b_last = cs_t[H_M + h:H_M + h + 1, L - 1:L]
        g_r = b_last - b_r + i_r
        m_new = jnp.maximum(b_last + m_prev, jnp.max(g_r, axis=1, keepdims=True))
        wk = jnp.exp(g_r - m_new) * scale
        decay = jnp.exp(b_last + m_prev - m_new)
        c_ref[h] = decay * c_state + _dot((v_t * wk).astype(BF16), k)
        n_ref[h:h + 1, :] = decay * n_row + _dot(jnp.broadcast_to(wk, (8, L)).astype(BF16), k)[0:1]
        m_ref[h:h + 1, :] = jnp.broadcast_to(m_new, (1, LANES))


def _mlstm(mqk, mv, smo, small, bias_row, norm_w, batch, seq):
    L = min(MLSTM_L, seq)
    r3 = lambda a: a.reshape(batch, seq, a.shape[-1])
    blk = lambda n: pl.BlockSpec((None, L, n), lambda b, c: (b, c, 0))
    tri = jnp.tril(jnp.ones((L, L), BF16))
    out = pl.pallas_call(
        _mlstm_kernel,
        grid=(batch, seq // L),
        in_specs=[blk(2 * H_M * DQK_M), blk(D_MV), blk(D_MV), blk(LANES), _const_spec((1, LANES)),
                  _const_spec((D_MV, L)), _const_spec((L, L))],
        out_specs=blk(D_MV),
        out_shape=jax.ShapeDtypeStruct((batch, seq, D_MV), BF16),
        scratch_shapes=[pltpu.VMEM((H_M, DV_M, DQK_M), F32), pltpu.VMEM((8, LANES), F32),
                        pltpu.VMEM((8, LANES), F32)],
        compiler_params=pltpu.CompilerParams(dimension_semantics=("arbitrary", "arbitrary"),
                                             vmem_limit_bytes=VMEM_LIMIT_BYTES),
        name="mlstm",
    )(r3(mqk), r3(mv), r3(smo), r3(small), bias_row, jnp.broadcast_to(norm_w.reshape(D_MV, 1), (D_MV, L)), tri)
    return out.reshape(batch * seq, D_MV)


def _dsa_kernel(aq_ref, iq_ref, small_ref, kvi_ref, tri_ref, eye_ref, o_ref,
                keys_ref, planes_ref, und_ref, s0_ref, s1_ref, p_ref, acc_ref, m_ref, *, n_sel):
    kc = keys_ref.shape[1]
    qb = pl.program_id(1)
    nkc = (qb * Q_BLOCK) // kc + 1
    t_idx = qb * Q_BLOCK + lax.broadcasted_iota(I32, (kc, Q_BLOCK), 1)
    s_iota = lax.broadcasted_iota(I32, (kc, Q_BLOCK), 0)

    def part(x):
        return x.reshape(kc // ACC_ROWS, ACC_ROWS, Q_BLOCK)

    def rows_sum(x):
        return jnp.sum(jnp.sum(part(x), axis=0), axis=0, keepdims=True)

    def head_rows(ref, heads, keep_low_only):
        low = lax.broadcasted_iota(I32, (Q_BLOCK, LANES), 1) < DH_A
        out = []
        for pair in range(heads // 2):
            blk = ref[:, pair * LANES:(pair + 1) * LANES].astype(F32)
            for x in (blk, pltpu.roll(blk, DH_A, axis=1)):
                out.append((jnp.where(low, x, 0.0) if keep_low_only else x).astype(BF16))
        return jnp.concatenate(out, axis=0)

    iq_all = head_rows(iq_ref, H_IDX, False)
    w_scale = (H_IDX ** -0.5) * (D_IDX ** -0.5)
    sm_t = small_ref[...].T
    w_rows = [sm_t[2 * H_M + h:2 * H_M + h + 1, :] * w_scale for h in range(H_IDX)]

    hk = KEY_GROUP
    vis_margin = (t_idx - s_iota)[0:hk]

    def make_keys(c, sub):
        s0 = c * kc + sub * hk
        ik = kvi_ref[pl.ds(pl.multiple_of(s0, hk), hk), LANES:2 * LANES]
        lg = _dot_nt(ik, iq_all)
        sc = w_rows[0] * jnp.maximum(lg[:, 0:Q_BLOCK], 0.0)
        for h in range(1, H_IDX):
            sc = sc + w_rows[h] * jnp.maximum(lg[:, h * Q_BLOCK:(h + 1) * Q_BLOCK], 0.0)
        bits = lax.bitcast_convert_type(sc, I32)
        bits = jnp.where(bits == INT_MIN, 0, bits)
        key = bits ^ ((bits >> 31) & 0x7FFFFFFF)
        key = jnp.where(vis_margin >= s0, key, INT_MIN)
        keys_ref[c, sub * hk:(sub + 1) * hk, :] = key
        ukey = key ^ INT_MIN
        words = [ukey[8 * j:8 * (j + 1)] for j in range(32)]
        step = 16
        for mask in TRANSPOSE_MASKS:
            for j in range(32):
                if j & step == 0:
                    hi = lax.shift_right_logical(words[j + step], jnp.full((8, Q_BLOCK), step, I32))
                    t = (words[j] ^ hi) & mask
                    words[j] = words[j] ^ t
                    words[j + step] = words[j + step] ^ jnp.left_shift(t, jnp.int32(step))
            step //= 2
        for i in range(32):
            planes_ref[c, i, 8 * sub:8 * (sub + 1), :] = words[i]

    def score_body(c, carry):
        for sub in range(kc // hk):
            make_keys(c, sub)
        und_ref[c] = jnp.full(und_ref.shape[1:], -1, I32)
        return carry

    lax.fori_loop(0, nkc, score_body, 0)

    def sweep(i_prev, flip, i_next):
        def body(c, acc):
            und = und_ref[c]
            if i_prev is not None:
                und = und & (planes_ref[c, i_prev] ^ flip)
                und_ref[c] = und
            if i_next is not None:
                und = und & planes_ref[c, i_next]
            return acc + lax.population_count(und)
        acc = lax.fori_loop(0, nkc, body, jnp.zeros(und_ref.shape[1:], I32))
        return jnp.sum(acc.astype(F32), axis=0, keepdims=True)

    def decide(i, n_set, t_bits, n_above):
        ok = n_above + n_set >= n_sel
        return (jnp.where(ok, t_bits | jnp.left_shift(jnp.int32(1), 31 - i), t_bits),
                jnp.where(ok, n_above, n_above + n_set), jnp.where(ok, 0, -1))

    def bit_body(i, state):
        t_bits, n_above, flip = state
        return decide(i, sweep(i - 1, flip, i), t_bits, n_above)

    state = decide(0, sweep(None, None, 0), jnp.zeros((1, Q_BLOCK), I32), jnp.zeros((1, Q_BLOCK), F32))
    t_bits, n_gt, flip = lax.fori_loop(1, 32, bit_body, state)
    n_eq = sweep(31, flip, None)
    thr = t_bits ^ INT_MIN
    need = n_sel - n_gt
    excess_ties = jnp.max(n_eq - need) > 0.0

    q_all = head_rows(aq_ref, H_A, True)
    q_aug = jnp.concatenate([q_all, eye_ref[...]], axis=1)
    m_ref[...] = jnp.full(m_ref.shape, M_FLOOR, F32)
    acc_ref[...] = jnp.zeros(acc_ref.shape, F32)
    ones_row = jnp.where(lax.broadcasted_iota(I32, (LANES - DH_A, kc), 0) == 0, 1.0, 0.0).astype(BF16)

    def logits(c, s_ref, tie_carry, with_ties):
        start = pl.multiple_of(c * kc, kc)
        k = keys_ref[c]
        if with_ties:
            eq = k == thr
            eq_f = jnp.where(eq, 1.0, 0.0)
            rank = _dot(tri_ref[...], eq_f.astype(BF16)) + tie_carry
            sel = ((k > thr) | (eq & (rank < need))) & (k > INT_MIN)
            tie_carry = tie_carry + rows_sum(eq_f)
        else:
            sel = (k >= thr) & (k > INT_MIN)
        bias = jnp.where(sel, 0.0, MASK_BIAS).astype(BF16)
        k_aug = jnp.concatenate([kvi_ref[pl.ds(start, kc), 0:LANES], bias], axis=1)
        s_ref[:, 0:H_A * Q_BLOCK] = _dot_nt(k_aug, q_aug)
        return tie_carry

    def accumulate(c, s_ref):
        start = pl.multiple_of(c * kc, kc)
        kv_t = kvi_ref[pl.ds(start, kc), 0:LANES].astype(F32).T
        v_t = jnp.concatenate([kv_t[DH_A:].astype(BF16), ones_row], axis=0)
        rescale = []
        for h in range(H_A):
            cols = slice(h * Q_BLOCK, (h + 1) * Q_BLOCK)
            sh = s_ref[:, cols]
            m_old = m_ref[h:h + 1, :]
            m_new = jnp.maximum(m_old, jnp.max(jnp.max(part(sh), axis=0), axis=0, keepdims=True))
            p_ref[:, cols] = jnp.exp2(sh - m_new).astype(BF16)
            m_ref[h:h + 1, :] = m_new
            rescale.append(jnp.exp2(m_old - m_new))
        acc_ref[:, 0:H_A * Q_BLOCK] = (acc_ref[:, 0:H_A * Q_BLOCK] * jnp.concatenate(rescale, axis=1)
                                       + _dot(v_t, p_ref[:, 0:H_A * Q_BLOCK]))

    def attend(with_ties):
        last = nkc - 1
        tie = logits(0, s0_ref, jnp.zeros((1, Q_BLOCK), F32), with_ties)

        def pair(j, tie):
            c = 2 * j
            tie = logits(c + 1, s1_ref, tie, with_ties)
            accumulate(c, s0_ref)
            tie = logits(c + 2, s0_ref, tie, with_ties)
            accumulate(c + 1, s1_ref)
            return tie

        n_pairs = last // 2
        tie = lax.fori_loop(0, n_pairs, pair, tie)
        first_left = 2 * n_pairs

        @pl.when(first_left == last)
        def _():
            accumulate(last, s0_ref)

        @pl.when(first_left < last)
        def _():
            logits(last, s1_ref, tie, with_ties)
            accumulate(first_left, s0_ref)
            accumulate(last, s1_ref)

    @pl.when(excess_ties)
    def _():
        attend(True)

    @pl.when(jnp.logical_not(excess_ties))
    def _():
        attend(False)

    lane = lax.broadcasted_iota(I32, (Q_BLOCK, LANES), 1)
    pairs = []
    for hp in range(H_A // 2):
        o = []
        for h in (2 * hp, 2 * hp + 1):
            cols = slice(h * Q_BLOCK, (h + 1) * Q_BLOCK)
            o.append((acc_ref[:, cols] / acc_ref[DH_A:DH_A + 1, cols]).T)
        pairs.append(jnp.where(lane < DH_A, o[0], pltpu.roll(o[1], DH_A, axis=1)))
    o_ref[...] = jnp.concatenate(pairs, axis=1).astype(BF16)


def _dsa(aq, iq, small, kvi, batch, seq):
    d = H_A * DH_A
    kc = min(DSA_KC, seq)
    assert kc % KEY_GROUP == 0 and seq % kc == 0
    n_sel = min(TOPK_MAX, seq // 4)
    r3 = lambda a: a.reshape(batch, seq, a.shape[-1])
    qblk = lambda n: pl.BlockSpec((None, Q_BLOCK, n), lambda b, q: (b, q, 0))
    tri = jnp.tril(jnp.ones((kc, kc), BF16), k=-1)
    eye = jnp.tile(jnp.eye(Q_BLOCK, dtype=BF16), (H_A, 1))
    out = pl.pallas_call(
        functools.partial(_dsa_kernel, n_sel=n_sel),
        grid=(batch, seq // Q_BLOCK),
        in_specs=[qblk(H_A * DH_A), qblk(H_IDX * D_IDX), qblk(LANES),
                  pl.BlockSpec((None, seq, 2 * LANES), lambda b, q: (b, 0, 0)),
                  _const_spec((kc, kc)), _const_spec(eye.shape)],
        out_specs=qblk(d),
        out_shape=jax.ShapeDtypeStruct((batch, seq, d), BF16),
        scratch_shapes=[pltpu.VMEM((seq // kc, kc, Q_BLOCK), I32),
                        pltpu.VMEM((seq // kc, 32, kc // 32, Q_BLOCK), I32),
                        pltpu.VMEM((seq // kc, kc // 32, Q_BLOCK), I32),
                        pltpu.VMEM((kc, H_A * Q_BLOCK + LANES), F32),
                        pltpu.VMEM((kc, H_A * Q_BLOCK + LANES), F32),
                        pltpu.VMEM((kc, H_A * Q_BLOCK + LANES), BF16),
                        pltpu.VMEM((LANES, H_A * Q_BLOCK + LANES), F32),
                        pltpu.VMEM((H_A, Q_BLOCK), F32)],
        compiler_params=pltpu.CompilerParams(dimension_semantics=("arbitrary", "arbitrary"),
                                             vmem_limit_bytes=VMEM_LIMIT_BYTES),
        name="dsa",
    )(r3(aq), r3(iq), r3(small), r3(kvi), tri, eye)
    return out.reshape(batch * seq, d)


def _mix_out_kernel(x_ref, ya_ref, hm_ref, ha_ref, g12_ref, wm_ref, wa_ref, wo_ref, o_ref):
    d = x_ref.shape[1]
    ym = _dot(hm_ref[...], wm_ref[...])
    yc = _dot(ha_ref[...], wa_ref[...])
    merged = ya_ref[...].astype(F32) + g12_ref[:, 0:d].astype(F32) * ym + g12_ref[:, d:2 * d].astype(F32) * yc
    o_ref[...] = x_ref[...] + _dot(merged.astype(BF16), wo_ref[...])


def _mix_out(x, ya, hm, ha, g12, wm, wa, wo):
    t, d = x.shape
    tm = min(TM_FFN, t)
    tok = lambda i: (i, 0)
    return pl.pallas_call(
        _mix_out_kernel,
        grid=(t // tm,),
        in_specs=[pl.BlockSpec((tm, d), tok), pl.BlockSpec((tm, d), tok), pl.BlockSpec((tm, D_MV), tok),
                  pl.BlockSpec((tm, ha.shape[1]), tok), pl.BlockSpec((tm, 2 * d), tok), _const_spec(wm.shape),
                  _const_spec(wa.shape), _const_spec(wo.shape)],
        out_specs=pl.BlockSpec((tm, d), tok),
        out_shape=jax.ShapeDtypeStruct((t, d), F32),
        compiler_params=pltpu.CompilerParams(dimension_semantics=("arbitrary",),
                                             vmem_limit_bytes=VMEM_LIMIT_BYTES),
        name="mix_out",
    )(x, ya, hm, ha, g12, wm, wa, wo)


def _mixer_weights(w_in):
    d = w_in.shape[0]
    wb = w_in.astype(BF16)
    z = lambda n: jnp.zeros((d, n), BF16)
    col = lambda o, n: wb[:, o:o + n]
    return {
        "conv": col(O_CB, 3 * D_CONV),
        "mqk": col(O_MQ, 2 * H_M * DQK_M),
        "mv": col(O_MV, D_MV),
        "mo": col(O_MO, D_MV),
        "small": jnp.concatenate([col(O_MI, 2 * H_M), col(O_IW, H_IDX), z(LANES - 2 * H_M - H_IDX)], axis=1),
        "aq": col(O_AQ, H_A * DH_A) * jnp.asarray(DH_A ** -0.5, BF16),
        "kvi": jnp.concatenate([col(O_AK, DH_A), col(O_AV, DH_A), col(O_IK, D_IDX), z(LANES - D_IDX)], axis=1),
        "iq": col(O_IQ, H_IDX * D_IDX),
        "gates": col(O_GATES, 3 * d),
    }


def kernel(x, p, norm_ffn1, ffn1_w_gu, ffn1_w_down, norm_mix, w_in, conv_w, conv_w_out, mlstm_b_i, mlstm_b_f,
           mlstm_norm, mlstm_w_out, attn_w_out, w_o, norm_ffn2, ffn2_w_gu, ffn2_w_down, norm_ple, ple_w_gate,
           ple_w_proj, final_norm):
    batch, seq, d = x.shape
    depth = w_in.shape[0]
    t = batch * seq
    xs = x.reshape(t, d)
    for l in range(depth):
        xs = _ffn(xs, norm_ffn1[l], ffn1_w_gu[l].astype(BF16), ffn1_w_down[l].astype(BF16))
        pr = _mix_in(xs, norm_mix[l], _mixer_weights(w_in[l]), conv_w[l], conv_w_out[l].astype(BF16), seq)
        bias_row = jnp.concatenate([mlstm_b_i[l], mlstm_b_f[l], jnp.zeros((LANES - 2 * H_M,), F32)]).reshape(1, LANES)
        hm = _mlstm(pr["mqk"], pr["mv"], pr["smo"], pr["small"], bias_row, mlstm_norm[l], batch, seq)
        ha = _dsa(pr["aq"], pr["iq"], pr["small"], pr["kvi"], batch, seq)
        xs = _mix_out(xs, pr["ya"], hm, ha, pr["g12"], mlstm_w_out[l].astype(BF16), attn_w_out[l].astype(BF16),
                      w_o[l].astype(BF16))
        xs = _ffn(xs, norm_ffn2[l], ffn2_w_gu[l].astype(BF16), ffn2_w_down[l].astype(BF16),
                  ple=(p.reshape(depth, t, p.shape[-1]), l, norm_ple[l], ple_w_gate[l].astype(BF16),
                       ple_w_proj[l].astype(BF16)),
                  final_g=final_norm if l == depth - 1 else None)
    return xs.reshape(batch, seq, d)
```

```python
import functools

import jax
import jax.numpy as jnp
from jax import lax
from jax.experimental import pallas as pl
from jax.experimental.pallas import tpu as pltpu

BF16 = jnp.bfloat16
F32 = jnp.float32
I32 = jnp.int32

D_PLE = 256
D_FF = 2816
D_CONV = 512
CONV_W = 3
H_M = 4
DQK_M = 128
DV_M = 256
D_MV = H_M * DV_M
H_A = 8
DH_A = 64
H_IDX = 4
D_IDX = 64
TOPK_MAX = 256
Q_BLOCK = 128
EPS = 1e-6

_SPLITS = (D_CONV, D_CONV, D_CONV, H_M * DQK_M, H_M * DQK_M, D_MV, D_MV, H_M, H_M,
           H_A * DH_A, DH_A, DH_A, H_IDX * D_IDX, D_IDX, H_IDX)
_OFF = [0]
for _s in _SPLITS:
    _OFF.append(_OFF[-1] + _s)
(O_CB, O_CC, O_CX, O_MQ, O_MK, O_MV, O_MO, O_MI, O_MF, O_AQ, O_AK, O_AV, O_IQ, O_IK, O_IW, O_GATES) = _OFF

LANES = 128
VMEM_LIMIT_BYTES = 56 * 1024 * 1024

TM_FFN = 1024
TM_MIX = 512
FF_CHUNK = 256
MLSTM_L = 256
DSA_KC = 512
INT_MIN = -2 ** 31
MASK_BIAS = -1e30
M_FLOOR = -1e29
LOG2E = 1.4426950408889634
ACC_ROWS = 64
NROWS = 16
KEY_GROUP = 256
TRANSPOSE_MASKS = (0x0000FFFF, 0x00FF00FF, 0x0F0F0F0F, 0x33333333, 0x55555555)


def _const_spec(shape):
    nd = len(shape)
    return pl.BlockSpec(shape, lambda *_: (0,) * nd, pipeline_mode=pl.Buffered(1))


def _rms(x, g):
    return x * lax.rsqrt(jnp.mean(x * x, axis=-1, keepdims=True) + EPS) * g


def _dot(a, b):
    return jnp.dot(a, b, preferred_element_type=F32)


def _dot_nt(a, b):
    return lax.dot_general(a, b, (((1,), (1,)), ((), ())), preferred_element_type=F32)


def _ffn_kernel(*refs, chunks, d_ff, with_ple, with_final):
    if with_ple:
        x_ref, g_ref, wgu_ref, wd_ref, p_ref, gp_ref, wpg_ref, wpp_ref = refs[:8]
        rest = refs[8:]
    else:
        x_ref, g_ref, wgu_ref, wd_ref = refs[:4]
        rest = refs[4:]
    if with_final:
        gf_ref, o_ref = rest
    else:
        (o_ref,) = rest
    x = x_ref[...]
    h = _rms(x, g_ref[...]).astype(BF16)
    acc = jnp.zeros_like(x)
    for off, w in chunks:
        gg = _dot(h, wgu_ref[:, off:off + w])
        uu = _dot(h, wgu_ref[:, d_ff + off:d_ff + off + w])
        a = (gg * jax.nn.sigmoid(gg) * uu).astype(BF16)
        acc = acc + _dot(a, wd_ref[off:off + w, :])
    x = x + 0.5 * acc
    if with_ple:
        gate = jax.nn.sigmoid(_dot(_rms(x, gp_ref[...]).astype(BF16), wpg_ref[...]))
        x = x + gate * _dot(p_ref[...].astype(BF16), wpp_ref[...])
    if with_final:
        x = _rms(x, gf_ref[...])
    o_ref[...] = x


def _ffn(x, g, wgu, wd, ple=None, final_g=None):
    t, d = x.shape
    d_ff = wd.shape[0]
    tm = min(TM_FFN, t)
    chunks = []
    off = 0
    while off < d_ff:
        w = min(FF_CHUNK, d_ff - off)
        chunks.append((off, w))
        off += w
    tok = lambda i: (i, 0)
    in_specs = [pl.BlockSpec((tm, d), tok), _const_spec((1, d)), _const_spec(wgu.shape), _const_spec(wd.shape)]
    args = [x, g.reshape(1, d), wgu, wd]
    if ple is not None:
        p, layer, gp, wpg, wpp = ple
        in_specs += [pl.BlockSpec((None, tm, p.shape[2]), lambda i: (layer, i, 0)), _const_spec((1, d)),
                     _const_spec(wpg.shape), _const_spec(wpp.shape)]
        args += [p, gp.reshape(1, d), wpg, wpp]
    if final_g is not None:
        in_specs.append(_const_spec((1, d)))
        args.append(final_g.reshape(1, d))
    kern = functools.partial(_ffn_kernel, chunks=tuple(chunks), d_ff=d_ff, with_ple=ple is not None,
                             with_final=final_g is not None)
    return pl.pallas_call(
        kern,
        grid=(t // tm,),
        in_specs=in_specs,
        out_specs=pl.BlockSpec((tm, d), tok),
        out_shape=jax.ShapeDtypeStruct((t, d), F32),
        compiler_params=pltpu.CompilerParams(dimension_semantics=("arbitrary",),
                                             vmem_limit_bytes=VMEM_LIMIT_BYTES),
        name="ffn",
    )(*args)


def _mix_in_kernel(x_ref, g_ref, wconv_ref, wmqk_ref, wmv_ref, wmo_ref, wsmall_ref, waq_ref, wkvi_ref, wiq_ref,
                   wgate_ref, cw_ref, cwo_ref,
                   ya_ref, mqk_ref, mv_ref, smo_ref, small_ref, aq_ref, kvi_ref, iq_ref, g12_ref,
                   ubuf_ref, *, tiles_per_seq):
    tm, d = x_ref.shape
    h = _rms(x_ref[...], g_ref[...]).astype(BF16)

    mqk_ref[...] = _dot(h, wmqk_ref[...]).astype(BF16)
    mv_ref[...] = _dot(h, wmv_ref[...]).astype(BF16)
    smo_ref[...] = jax.nn.sigmoid(_dot(h, wmo_ref[...])).astype(BF16)
    small_ref[...] = _dot(h, wsmall_ref[...])
    aq_ref[...] = (_dot(h, waq_ref[...]) * LOG2E).astype(BF16)
    kvi_ref[...] = _dot(h, wkvi_ref[...]).astype(BF16)
    iq_ref[...] = _dot(h, wiq_ref[...]).astype(BF16)
    g12_ref[...] = jax.nn.sigmoid(_dot(h, wgate_ref[:, d:3 * d])).astype(BF16)

    @pl.when(pl.program_id(0) % tiles_per_seq == 0)
    def _():
        ubuf_ref[0:8, :] = jnp.zeros((8, D_CONV), F32)

    bcx = _dot(h, wconv_ref[...])
    cb = bcx[:, 0:D_CONV]
    u = bcx[:, D_CONV:2 * D_CONV] * bcx[:, 2 * D_CONV:3 * D_CONV]
    ubuf_ref[8:tm + 8, :] = u
    y = (ubuf_ref[6:tm + 6, :] * cw_ref[0:1, :] + ubuf_ref[7:tm + 7, :] * cw_ref[1:2, :] + u * cw_ref[2:3, :])
    ubuf_ref[0:8, :] = ubuf_ref[tm:tm + 8, :]
    ya = _dot((cb * y).astype(BF16), cwo_ref[...])
    ya_ref[...] = (jax.nn.sigmoid(_dot(h, wgate_ref[:, 0:d])) * ya).astype(BF16)


def _mix_in(x, g, w, conv_w, conv_w_out, seq):
    t, d = x.shape
    tm = min(TM_MIX, seq)
    tok = lambda i: (i, 0)
    weights = [w["conv"], w["mqk"], w["mv"], w["mo"], w["small"], w["aq"], w["kvi"], w["iq"], w["gates"],
               conv_w, conv_w_out]
    outs = [("ya", d, BF16), ("mqk", 2 * H_M * DQK_M, BF16), ("mv", D_MV, BF16), ("smo", D_MV, BF16),
            ("small", LANES, F32), ("aq", H_A * DH_A, BF16), ("kvi", 2 * LANES, BF16), ("iq", H_IDX * D_IDX, BF16),
            ("g12", 2 * d, BF16)]
    res = pl.pallas_call(
        functools.partial(_mix_in_kernel, tiles_per_seq=seq // tm),
        grid=(t // tm,),
        in_specs=[pl.BlockSpec((tm, d), tok), _const_spec((1, d))] + [_const_spec(a.shape) for a in weights],
        out_specs=[pl.BlockSpec((tm, n), tok) for _, n, _ in outs],
        out_shape=[jax.ShapeDtypeStruct((t, n), dt) for _, n, dt in outs],
        scratch_shapes=[pltpu.VMEM((tm + 8, D_CONV), F32)],
        compiler_params=pltpu.CompilerParams(dimension_semantics=("arbitrary",),
                                             vmem_limit_bytes=VMEM_LIMIT_BYTES),
        name="mix_in",
    )(x, g.reshape(1, d), *weights)
    return {name: r for (name, _, _), r in zip(outs, res)}


def _log_sigmoid(x):
    return jnp.minimum(x, 0.0) - jnp.log(1.0 + jnp.exp(-jnp.abs(x)))


def _mlstm_kernel(qk_ref, v_ref, smo_ref, small_ref, bias_ref, nw_ref, tri_ref, o_ref, c_ref, n_ref, m_ref):
    L = qk_ref.shape[0]
    scale = DQK_M ** -0.5

    @pl.when(pl.program_id(1) == 0)
    def _():
        c_ref[...] = jnp.zeros(c_ref.shape, F32)
        n_ref[...] = jnp.zeros(n_ref.shape, F32)
        m_ref[...] = jnp.zeros(m_ref.shape, F32)

    sm = small_ref[...] + bias_ref[...]
    lane = lax.broadcasted_iota(I32, sm.shape, 1)
    g = jnp.where((lane >= H_M) & (lane < 2 * H_M), _log_sigmoid(sm), sm)
    tri = tri_ref[...]
    g1 = g.astype(BF16)
    r1 = g - g1.astype(F32)
    g2 = r1.astype(BF16)
    g3 = (r1 - g2.astype(F32)).astype(BF16)
    cs3 = _dot(tri, jnp.concatenate([g1, g2, g3], axis=1))
    cs = cs3[:, 0:LANES] + cs3[:, LANES:2 * LANES] + cs3[:, 2 * LANES:3 * LANES]
    g_t = g.T
    cs_t = cs.T
    causal = lax.broadcasted_iota(I32, (L, L), 0) <= lax.broadcasted_iota(I32, (L, L), 1)

    def over_sources(x, op):
        return op(op(x.reshape(x.shape[0] // ACC_ROWS, ACC_ROWS, x.shape[1]), axis=0), axis=0, keepdims=True)

    for h in range(H_M):
        i_r = g_t[h:h + 1, :]
        b_r = cs_t[H_M + h:H_M + h + 1, :]
        a_c = g[:, h:h + 1] - cs[:, H_M + h:H_M + h + 1]
        m_prev = m_ref[h:h + 1, 0:1]
        dlog = jnp.where(causal, a_c + b_r, -jnp.inf)
        inter = b_r + m_prev
        m_t = jnp.maximum(inter, over_sources(dlog, jnp.max))
        q = qk_ref[:, h * DQK_M:(h + 1) * DQK_M]
        k = qk_ref[:, (H_M + h) * DQK_M:(H_M + h + 1) * DQK_M]
        v_t = v_ref[:, h * DV_M:(h + 1) * DV_M].astype(F32).T
        sc = _dot_nt(k, q) * scale * jnp.exp(dlog - m_t)
        w_int = jnp.exp(inter - m_t)
        c_state = c_ref[h]
        n_row = n_ref[h:h + 1, :]
        cn = jnp.concatenate([c_state, jnp.broadcast_to(n_row, (NROWS, DQK_M))], axis=0).astype(BF16)
        cq = _dot_nt(cn, q)
        num = w_int * cq[0:DV_M] + _dot(v_t.astype(BF16), sc.astype(BF16))
        den = w_int * cq[DV_M:DV_M + 1] + over_sources(sc, jnp.sum)
        hv = num / jnp.maximum(jnp.abs(den), jnp.exp(-m_t))
        hn = hv * lax.rsqrt(over_sources(hv * hv, jnp.sum) * (1.0 / DV_M) + EPS) * nw_ref[h * DV_M:(h + 1) * DV_M, :]
        o_ref[:, h * DV_M:(h + 1) * DV_M] = (smo_ref[:, h * DV_M:(h + 1) * DV_M].astype(F32) * hn.T).astype(BF16)
        b_last = cs_t[H_M + h:H_M + h + 1, L - 1:L]
        g_r = b_last - b_r + i_r
        m_new = jnp.maximum(b_last + m_prev, jnp.max(g_r, axis=1, keepdims=True))
        wk = jnp.exp(g_r - m_new) * scale
        decay = jnp.exp(b_last + m_prev - m_new)
        upd = _dot(jnp.concatenate([v_t * wk, jnp.broadcast_to(wk, (NROWS, L))], axis=0).astype(BF16), k)
        c_ref[h] = decay * c_state + upd[0:DV_M]
        n_ref[h:h + 1, :] = decay * n_row + upd[DV_M:DV_M + 1]
        m_ref[h:h + 1, :] = jnp.broadcast_to(m_new, (1, LANES))


def _mlstm(mqk, mv, smo, small, bias_row, norm_w, batch, seq):
    L = min(MLSTM_L, seq)
    r3 = lambda a: a.reshape(batch, seq, a.shape[-1])
    blk = lambda n: pl.BlockSpec((None, L, n), lambda b, c: (b, c, 0))
    tri = jnp.tril(jnp.ones((L, L), BF16))
    out = pl.pallas_call(
        _mlstm_kernel,
        grid=(batch, seq // L),
        in_specs=[blk(2 * H_M * DQK_M), blk(D_MV), blk(D_MV), blk(LANES), _const_spec((1, LANES)),
                  _const_spec((D_MV, L)), _const_spec((L, L))],
        out_specs=blk(D_MV),
        out_shape=jax.ShapeDtypeStruct((batch, seq, D_MV), BF16),
        scratch_shapes=[pltpu.VMEM((H_M, DV_M, DQK_M), F32), pltpu.VMEM((8, LANES), F32),
                        pltpu.VMEM((8, LANES), F32)],
        compiler_params=pltpu.CompilerParams(dimension_semantics=("arbitrary", "arbitrary"),
                                             vmem_limit_bytes=VMEM_LIMIT_BYTES),
        name="mlstm",
    )(r3(mqk), r3(mv), r3(smo), r3(small), bias_row, jnp.broadcast_to(norm_w.reshape(D_MV, 1), (D_MV, L)), tri)
    return out.reshape(batch * seq, D_MV)


def _dsa_kernel(aq_ref, iq_ref, small_ref, kvi_ref, tri_ref, eye_ref, o_ref,
                keys_ref, planes_ref, und_ref, s0_ref, s1_ref, p_ref, acc_ref, m_ref, *, n_sel):
    kc = keys_ref.shape[1]
    qb = pl.program_id(1)
    nkc = (qb * Q_BLOCK) // kc + 1
    t_idx = qb * Q_BLOCK + lax.broadcasted_iota(I32, (kc, Q_BLOCK), 1)
    s_iota = lax.broadcasted_iota(I32, (kc, Q_BLOCK), 0)

    def part(x):
        return x.reshape(kc // ACC_ROWS, ACC_ROWS, Q_BLOCK)

    def rows_sum(x):
        return jnp.sum(jnp.sum(part(x), axis=0), axis=0, keepdims=True)

    def head_rows(ref, heads, keep_low_only):
        low = lax.broadcasted_iota(I32, (Q_BLOCK, LANES), 1) < DH_A
        out = []
        for pair in range(heads // 2):
            blk = ref[:, pair * LANES:(pair + 1) * LANES].astype(F32)
            for x in (blk, pltpu.roll(blk, DH_A, axis=1)):
                out.append((jnp.where(low, x, 0.0) if keep_low_only else x).astype(BF16))
        return jnp.concatenate(out, axis=0)

    iq_all = head_rows(iq_ref, H_IDX, False)
    w_scale = (H_IDX ** -0.5) * (D_IDX ** -0.5)
    sm_t = small_ref[...].T
    w_rows = [sm_t[2 * H_M + h:2 * H_M + h + 1, :] * w_scale for h in range(H_IDX)]

    hk = KEY_GROUP
    vis_margin = (t_idx - s_iota)[0:hk]

    def make_keys(c, sub):
        s0 = c * kc + sub * hk
        ik = kvi_ref[pl.ds(pl.multiple_of(s0, hk), hk), LANES:2 * LANES]
        lg = _dot_nt(ik, iq_all)
        sc = w_rows[0] * jnp.maximum(lg[:, 0:Q_BLOCK], 0.0)
        for h in range(1, H_IDX):
            sc = sc + w_rows[h] * jnp.maximum(lg[:, h * Q_BLOCK:(h + 1) * Q_BLOCK], 0.0)
        bits = lax.bitcast_convert_type(sc, I32)
        bits = jnp.where(bits == INT_MIN, 0, bits)
        key = bits ^ ((bits >> 31) & 0x7FFFFFFF)
        key = jnp.where(vis_margin >= s0, key, INT_MIN)
        keys_ref[c, sub * hk:(sub + 1) * hk, :] = key
        ukey = key ^ INT_MIN
        words = [ukey[8 * j:8 * (j + 1)] for j in range(32)]
        step = 16
        for mask in TRANSPOSE_MASKS:
            for j in range(32):
                if j & step == 0:
                    hi = lax.shift_right_logical(words[j + step], jnp.full((8, Q_BLOCK), step, I32))
                    t = (words[j] ^ hi) & mask
                    words[j] = words[j] ^ t
                    words[j + step] = words[j + step] ^ jnp.left_shift(t, jnp.int32(step))
            step //= 2
        for i in range(32):
            planes_ref[c, i, 8 * sub:8 * (sub + 1), :] = words[i]

    def score_body(c, carry):
        for sub in range(kc // hk):
            make_keys(c, sub)
        und_ref[c] = jnp.full(und_ref.shape[1:], -1, I32)
        return carry

    lax.fori_loop(0, nkc, score_body, 0)

    def sweep(i_prev, flip, i_next):
        def body(c, acc):
            und = und_ref[c]
            if i_prev is not None:
                und = und & (planes_ref[c, i_prev] ^ flip)
                und_ref[c] = und
            if i_next is not None:
                und = und & planes_ref[c, i_next]
            return acc + lax.population_count(und)
        acc = lax.fori_loop(0, nkc, body, jnp.zeros(und_ref.shape[1:], I32))
        return jnp.sum(acc.astype(F32), axis=0, keepdims=True)

    def decide(i, n_set, t_bits, n_above):
        ok = n_above + n_set >= n_sel
        return (jnp.where(ok, t_bits | jnp.left_shift(jnp.int32(1), 31 - i), t_bits),
                jnp.where(ok, n_above, n_above + n_set), jnp.where(ok, 0, -1))

    def bit_body(i, state):
        t_bits, n_above, flip = state
        return decide(i, sweep(i - 1, flip, i), t_bits, n_above)

    state = decide(0, sweep(None, None, 0), jnp.zeros((1, Q_BLOCK), I32), jnp.zeros((1, Q_BLOCK), F32))
    t_bits, n_gt, flip = lax.fori_loop(1, 32, bit_body, state)
    n_eq = sweep(31, flip, None)
    thr = t_bits ^ INT_MIN
    need = n_sel - n_gt
    excess_ties = jnp.max(n_eq - need) > 0.0

    q_all = head_rows(aq_ref, H_A, True)
    q_aug = jnp.concatenate([q_all, eye_ref[...]], axis=1)
    m_ref[...] = jnp.full(m_ref.shape, M_FLOOR, F32)
    acc_ref[...] = jnp.zeros(acc_ref.shape, F32)
    ones_row = jnp.where(lax.broadcasted_iota(I32, (LANES - DH_A, kc), 0) == 0, 1.0, 0.0).astype(BF16)

    def logits(c, s_ref, tie_carry, with_ties):
        start = pl.multiple_of(c * kc, kc)
        k = keys_ref[c]
        if with_ties:
            eq = k == thr
            eq_f = jnp.where(eq, 1.0, 0.0)
            rank = _dot(tri_ref[...], eq_f.astype(BF16)) + tie_carry
            sel = ((k > thr) | (eq & (rank < need))) & (k > INT_MIN)
            tie_carry = tie_carry + rows_sum(eq_f)
        else:
            sel = (k >= thr) & (k > INT_MIN)
        bias = jnp.where(sel, 0.0, MASK_BIAS).astype(BF16)
        k_aug = jnp.concatenate([kvi_ref[pl.ds(start, kc), 0:LANES], bias], axis=1)
        s_ref[:, 0:H_A * Q_BLOCK] = _dot_nt(k_aug, q_aug)
        return tie_carry

    def accumulate(c, s_ref):
        start = pl.multiple_of(c * kc, kc)
        kv_t = kvi_ref[pl.ds(start, kc), 0:LANES].astype(F32).T
        v_t = jnp.concatenate([kv_t[DH_A:].astype(BF16), ones_row], axis=0)
        rescale = []
        for h in range(H_A):
            cols = slice(h * Q_BLOCK, (h + 1) * Q_BLOCK)
            sh = s_ref[:, cols]
            m_old = m_ref[h:h + 1, :]
            m_new = jnp.maximum(m_old, jnp.max(jnp.max(part(sh), axis=0), axis=0, keepdims=True))
            p_ref[:, cols] = jnp.exp2(sh - m_new).astype(BF16)
            m_ref[h:h + 1, :] = m_new
            rescale.append(jnp.exp2(m_old - m_new))
        acc_ref[:, 0:H_A * Q_BLOCK] = (acc_ref[:, 0:H_A * Q_BLOCK] * jnp.concatenate(rescale, axis=1)
                                       + _dot(v_t, p_ref[:, 0:H_A * Q_BLOCK]))

    def attend(with_ties):
        last = nkc - 1
        tie = logits(0, s0_ref, jnp.zeros((1, Q_BLOCK), F32), with_ties)

        def pair(j, tie):
            c = 2 * j
            tie = logits(c + 1, s1_ref, tie, with_ties)
            accumulate(c, s0_ref)
            tie = logits(c + 2, s0_ref, tie, with_ties)
            accumulate(c + 1, s1_ref)
            return tie

        n_pairs = last // 2
        tie = lax.fori_loop(0, n_pairs, pair, tie)
        first_left = 2 * n_pairs

        @pl.when(first_left == last)
        def _():
            accumulate(last, s0_ref)

        @pl.when(first_left < last)
        def _():
            logits(last, s1_ref, tie, with_ties)
            accumulate(first_left, s0_ref)
            accumulate(last, s1_ref)

    @pl.when(excess_ties)
    def _():
        attend(True)

    @pl.when(jnp.logical_not(excess_ties))
    def _():
        attend(False)

    lane = lax.broadcasted_iota(I32, (Q_BLOCK, LANES), 1)
    pairs = []
    for hp in range(H_A // 2):
        o = []
        for h in (2 * hp, 2 * hp + 1):
            cols = slice(h * Q_BLOCK, (h + 1) * Q_BLOCK)
            o.append((acc_ref[:, cols] / acc_ref[DH_A:DH_A + 1, cols]).T)
        pairs.append(jnp.where(lane < DH_A, o[0], pltpu.roll(o[1], DH_A, axis=1)))
    o_ref[...] = jnp.concatenate(pairs, axis=1).astype(BF16)


def _dsa(aq, iq, small, kvi, batch, seq):
    d = H_A * DH_A
    kc = min(DSA_KC, seq)
    assert kc % KEY_GROUP == 0 and seq % kc == 0
    n_sel = min(TOPK_MAX, seq // 4)
    r3 = lambda a: a.reshape(batch, seq, a.shape[-1])
    qblk = lambda n: pl.BlockSpec((None, Q_BLOCK, n), lambda b, q: (b, q, 0))
    tri = jnp.tril(jnp.ones((kc, kc), BF16), k=-1)
    eye = jnp.tile(jnp.eye(Q_BLOCK, dtype=BF16), (H_A, 1))
    out = pl.pallas_call(
        functools.partial(_dsa_kernel, n_sel=n_sel),
        grid=(batch, seq // Q_BLOCK),
        in_specs=[qblk(H_A * DH_A), qblk(H_IDX * D_IDX), qblk(LANES),
                  pl.BlockSpec((None, seq, 2 * LANES), lambda b, q: (b, 0, 0)),
                  _const_spec((kc, kc)), _const_spec(eye.shape)],
        out_specs=qblk(d),
        out_shape=jax.ShapeDtypeStruct((batch, seq, d), BF16),
        scratch_shapes=[pltpu.VMEM((seq // kc, kc, Q_BLOCK), I32),
                        pltpu.VMEM((seq // kc, 32, kc // 32, Q_BLOCK), I32),
                        pltpu.VMEM((seq // kc, kc // 32, Q_BLOCK), I32),
                        pltpu.VMEM((kc, H_A * Q_BLOCK + LANES), F32),
                        pltpu.VMEM((kc, H_A * Q_BLOCK + LANES), F32),
                        pltpu.VMEM((kc, H_A * Q_BLOCK + LANES), BF16),
                        pltpu.VMEM((LANES, H_A * Q_BLOCK + LANES), F32),
                        pltpu.VMEM((H_A, Q_BLOCK), F32)],
        compiler_params=pltpu.CompilerParams(dimension_semantics=("arbitrary", "arbitrary"),
                                             vmem_limit_bytes=VMEM_LIMIT_BYTES),
        name="dsa",
    )(r3(aq), r3(iq), r3(small), r3(kvi), tri, eye)
    return out.reshape(batch * seq, d)


def _mix_out_kernel(x_ref, ya_ref, hm_ref, ha_ref, g12_ref, wm_ref, wa_ref, wo_ref, o_ref):
    d = x_ref.shape[1]
    ym = _dot(hm_ref[...], wm_ref[...])
    yc = _dot(ha_ref[...], wa_ref[...])
    merged = ya_ref[...].astype(F32) + g12_ref[:, 0:d].astype(F32) * ym + g12_ref[:, d:2 * d].astype(F32) * yc
    o_ref[...] = x_ref[...] + _dot(merged.astype(BF16), wo_ref[...])


def _mix_out(x, ya, hm, ha, g12, wm, wa, wo):
    t, d = x.shape
    tm = min(TM_FFN, t)
    tok = lambda i: (i, 0)
    return pl.pallas_call(
        _mix_out_kernel,
        grid=(t // tm,),
        in_specs=[pl.BlockSpec((tm, d), tok), pl.BlockSpec((tm, d), tok), pl.BlockSpec((tm, D_MV), tok),
                  pl.BlockSpec((tm, ha.shape[1]), tok), pl.BlockSpec((tm, 2 * d), tok), _const_spec(wm.shape),
                  _const_spec(wa.shape), _const_spec(wo.shape)],
        out_specs=pl.BlockSpec((tm, d), tok),
        out_shape=jax.ShapeDtypeStruct((t, d), F32),
        compiler_params=pltpu.CompilerParams(dimension_semantics=("arbitrary",),
                                             vmem_limit_bytes=VMEM_LIMIT_BYTES),
        name="mix_out",
    )(x, ya, hm, ha, g12, wm, wa, wo)


def _mixer_weights(w_in):
    d = w_in.shape[0]
    wb = w_in.astype(BF16)
    z = lambda n: jnp.zeros((d, n), BF16)
    col = lambda o, n: wb[:, o:o + n]
    return {
        "conv": col(O_CB, 3 * D_CONV),
        "mqk": col(O_MQ, 2 * H_M * DQK_M),
        "mv": col(O_MV, D_MV),
        "mo": col(O_MO, D_MV),
        "small": jnp.concatenate([col(O_MI, 2 * H_M), col(O_IW, H_IDX), z(LANES - 2 * H_M - H_IDX)], axis=1),
        "aq": col(O_AQ, H_A * DH_A) * jnp.asarray(DH_A ** -0.5, BF16),
        "kvi": jnp.concatenate([col(O_AK, DH_A), col(O_AV, DH_A), col(O_IK, D_IDX), z(LANES - D_IDX)], axis=1),
        "iq": col(O_IQ, H_IDX * D_IDX),
        "gates": col(O_GATES, 3 * d),
    }


def kernel(x, p, norm_ffn1, ffn1_w_gu, ffn1_w_down, norm_mix, w_in, conv_w, conv_w_out, mlstm_b_i, mlstm_b_f,
           mlstm_norm, mlstm_w_out, attn_w_out, w_o, norm_ffn2, ffn2_w_gu, ffn2_w_down, norm_ple, ple_w_gate,
           ple_w_proj, final_norm):
    batch, seq, d = x.shape
    depth = w_in.shape[0]
    t = batch * seq
    xs = x.reshape(t, d)
    for l in range(depth):
        xs = _ffn(xs, norm_ffn1[l], ffn1_w_gu[l].astype(BF16), ffn1_w_down[l].astype(BF16))
        pr = _mix_in(xs, norm_mix[l], _mixer_weights(w_in[l]), conv_w[l], conv_w_out[l].astype(BF16), seq)
        bias_row = jnp.concatenate([mlstm_b_i[l], mlstm_b_f[l], jnp.zeros((LANES - 2 * H_M,), F32)]).reshape(1, LANES)
        hm = _mlstm(pr["mqk"], pr["mv"], pr["smo"], pr["small"], bias_row, mlstm_norm[l], batch, seq)
        ha = _dsa(pr["aq"], pr["iq"], pr["small"], pr["kvi"], batch, seq)
        xs = _mix_out(xs, pr["ya"], hm, ha, pr["g12"], mlstm_w_out[l].astype(BF16), attn_w_out[l].astype(BF16),
                      w_o[l].astype(BF16))
        xs = _ffn(xs, norm_ffn2[l], ffn2_w_gu[l].astype(BF16), ffn2_w_down[l].astype(BF16),
                  ple=(p.reshape(depth, t, p.shape[-1]), l, norm_ple[l], ple_w_gate[l].astype(BF16),
                       ple_w_proj[l].astype(BF16)),
                  final_g=final_norm if l == depth - 1 else None)
    return xs.reshape(batch, seq, d)
```
